```python
import math
import jax
import jax.numpy as jnp
from jax import lax
import numpy as np

D_MODEL = 1024
BATCH = 8
SEQ = 2048
DEPTH = 2

GRID_W = 64
CTX_LEN = 256
HEAD_DIM = 64
DIFF_HEADS = 4
DIFF_V_DIM = 2 * HEAD_DIM
NA_HEADS = 8
NA_KH = 8
NA_KW = 16
Q_BLOCK = 128
ROPE_BASE = 10000.0
ROPE_AXIS_DIM = HEAD_DIM // 2
SSM_GROUP = 16
SSM_GROUPS = D_MODEL // SSM_GROUP
SSM_STATE = 64
MLP_HIDDEN = 4 * D_MODEL
N_EVEN = (DEPTH + 1) // 2
N_ODD = DEPTH // 2
A_QK = DIFF_HEADS * 2 * HEAD_DIM
A_V = DIFF_HEADS * DIFF_V_DIM
B_QKV = NA_HEADS * HEAD_DIM
IN_PROJ = 2 * A_QK + A_V + 3 * B_QKV
MIX_OUT = A_V + B_QKV
NORM_EPS = 1e-6
SUBLN_EPS = 1e-5

kernel_name = 'hybrid_diffattn_natten_s5_block'


def _rmsnorm(x, g, eps):
    xf = x.astype(jnp.float32)
    y = xf * lax.rsqrt(jnp.mean(xf * xf, axis=-1, keepdims=True) + eps)
    return (y * g.astype(jnp.float32)).astype(x.dtype)


def _rotate(x, cos, sin):
    half = x.shape[-1] // 2
    x1, x2 = x[..., :half], x[..., half:]
    cos = cos.astype(x.dtype)
    sin = sin.astype(x.dtype)
    return jnp.concatenate([x1 * cos - x2 * sin, x2 * cos + x1 * sin], axis=-1)


def _rope2d(x, cos_r, sin_r, cos_c, sin_c):
    return jnp.concatenate([_rotate(x[..., :ROPE_AXIS_DIM], cos_r, sin_r),
                            _rotate(x[..., ROPE_AXIS_DIM:], cos_c, sin_c)], axis=-1)


def _rope_tables(seq):
    t = jnp.arange(seq)
    row = (t // GRID_W).astype(jnp.float32)
    col = (t % GRID_W).astype(jnp.float32)
    n_freq = ROPE_AXIS_DIM // 2
    inv_freq = ROPE_BASE ** (-jnp.arange(n_freq, dtype=jnp.float32) / n_freq)
    ang_r = (row[:, None] * inv_freq[None, :]).reshape(seq, 1, 1, n_freq)
    ang_c = (col[:, None] * inv_freq[None, :]).reshape(seq, 1, 1, n_freq)
    return (jnp.cos(ang_r), jnp.sin(ang_r), jnp.cos(ang_c), jnp.sin(ang_c))


def _split_proj(p):
    b, n, _ = p.shape
    qa = p[..., 0:A_QK].reshape(b, n, DIFF_HEADS, 2, HEAD_DIM)
    ka = p[..., A_QK:2 * A_QK].reshape(b, n, DIFF_HEADS, 2, HEAD_DIM)
    va = p[..., 2 * A_QK:2 * A_QK + A_V].reshape(b, n, DIFF_HEADS, DIFF_V_DIM)
    base = 2 * A_QK + A_V
    qn = p[..., base:base + B_QKV].reshape(b, n, NA_HEADS, HEAD_DIM)
    kn = p[..., base + B_QKV:base + 2 * B_QKV].reshape(b, n, NA_HEADS, HEAD_DIM)
    vn = p[..., base + 2 * B_QKV:base + 3 * B_QKV].reshape(b, n, NA_HEADS, HEAD_DIM)
    return qa, ka, va, qn, kn, vn


def _diff_attend(q, k, v, lam, lam_init, g):
    s = jnp.einsum('bqhmd,bkhmd->bhmqk', q, k).astype(jnp.float32) * (HEAD_DIM ** -0.5)
    p = jax.nn.softmax(s, axis=-1)
    a = p[:, :, 0] - lam * p[:, :, 1]
    o = jnp.einsum('bhqk,bkhe->bqhe', a.astype(v.dtype), v)
    return _rmsnorm(o, g, SUBLN_EPS) * (1.0 - lam_init)


def _dense_attend(q, k, v):
    s = jnp.einsum('bqhd,bkhd->bhqk', q, k).astype(jnp.float32) * (HEAD_DIM ** -0.5)
    p = jax.nn.softmax(s, axis=-1)
    return jnp.einsum('bhqk,bkhd->bqhd', p.astype(v.dtype), v)


def _na_latent(qn, kn, vn, kc, vc, rpb, rows):
    b, s, h, d = qn.shape

    def grid(t):
        return t.reshape(b, rows, GRID_W, h, d).transpose(0, 3, 1, 2, 4)

    qg, kg, vg = grid(qn), grid(kn), grid(vn)
    kc_h = kc.transpose(0, 2, 1, 3)
    vc_h = vc.transpose(0, 2, 1, 3)
    kh = min(NA_KH, rows)
    kw = NA_KW
    col_q = jnp.arange(GRID_W)
    col_start = jnp.clip(col_q - kw // 2, 0, GRID_W - kw)
    col_idx = col_start[:, None] + jnp.arange(kw)[None, :]
    col_off = col_idx - col_q[:, None] + (NA_KW - 1)
    rpb_cols = rpb[:, :, col_off]
    scale = HEAD_DIM ** -0.5
    n_win = kh * kw

    def row_block(r):
        r_start = jnp.clip(r - kh // 2, 0, rows - kh)
        row_off = r_start + jnp.arange(kh) - r + (NA_KH - 1)
        q_r = lax.dynamic_index_in_dim(qg, r, axis=2, keepdims=False)
        k_band = lax.dynamic_slice_in_dim(kg, r_start, kh, axis=2)
        v_band = lax.dynamic_slice_in_dim(vg, r_start, kh, axis=2)
        k_win = k_band[:, :, :, col_idx]
        v_win = v_band[:, :, :, col_idx]
        bias = jnp.take(rpb_cols, row_off, axis=1).transpose(0, 2, 1, 3)
        s_win = jnp.einsum('bhqd,bhrqcd->bhqrc', q_r, k_win).astype(jnp.float32) * scale + bias.astype(jnp.float32)[None]
        s_ctx = jnp.einsum('bhqd,bhkd->bhqk', q_r, kc_h).astype(jnp.float32) * scale
        p = jax.nn.softmax(jnp.concatenate([s_win.reshape(b, h, GRID_W, n_win), s_ctx], axis=-1), axis=-1)
        p_win = p[..., :n_win].reshape(b, h, GRID_W, kh, kw).astype(vg.dtype)
        p_ctx = p[..., n_win:].astype(vg.dtype)
        return (jnp.einsum('bhqrc,bhrqcd->bhqd', p_win, v_win)
                + jnp.einsum('bhqk,bhkd->bhqd', p_ctx, vc_h))

    o = lax.map(row_block, jnp.arange(rows))
    return o.transpose(1, 0, 3, 2, 4).reshape(b, s, h * d)


def _even_mixer(h, hc, w_in_e, w_out_e, lq1, lk1, lq2, lk2, g_sub, rpb, rope, lam_init, rows, need_ctx):
    b, s, _ = h.shape
    lc = hc.shape[1]
    qa, ka, va, qn, kn, vn = _split_proj(h @ w_in_e)
    qac, kac, vac, qnc, knc, vnc = _split_proj(hc @ w_in_e)
    qa = _rope2d(qa, *rope)
    ka = _rope2d(ka, *rope)
    lam = (jnp.exp(jnp.sum(lq1.astype(jnp.float32) * lk1.astype(jnp.float32)))
           - jnp.exp(jnp.sum(lq2.astype(jnp.float32) * lk2.astype(jnp.float32))) + lam_init)
    k_all = jnp.concatenate([ka, kac], axis=1)
    v_all = jnp.concatenate([va, vac], axis=1)
    nb = s // Q_BLOCK
    qb = qa.reshape(b, nb, Q_BLOCK, DIFF_HEADS, 2, HEAD_DIM).swapaxes(0, 1)
    oa = lax.map(lambda qq: _diff_attend(qq, k_all, v_all, lam, lam_init, g_sub), qb)
    oa = oa.swapaxes(0, 1).reshape(b, s, A_V)
    ob = _na_latent(qn, kn, vn, knc, vnc, rpb, rows)
    y = jnp.concatenate([oa, ob], axis=-1) @ w_out_e
    yc = None
    if need_ctx:
        oac = _diff_attend(qac, kac, vac, lam, lam_init, g_sub).reshape(b, lc, A_V)
        obc = _dense_attend(qnc, knc, vnc).reshape(b, lc, B_QKV)
        yc = jnp.concatenate([oac, obc], axis=-1) @ w_out_e
    return y, yc


def _zoh(lam_re, lam_im, log_step, b_re, b_im):
    lr = lam_re.astype(jnp.float32)
    li = lam_im.astype(jnp.float32)
    delta = jnp.exp(log_step.astype(jnp.float32))[:, None]
    mag = jnp.exp(lr * delta)
    ar = mag * jnp.cos(li * delta)
    ai = mag * jnp.sin(li * delta)
    den = lr * lr + li * li
    nr = ar - 1.0
    cr = (nr * lr + ai * li) / den
    ci = (ai * lr - nr * li) / den
    br = b_re.astype(jnp.float32)
    bi = b_im.astype(jnp.float32)
    bbr = cr[..., None] * br - ci[..., None] * bi
    bbi = cr[..., None] * bi + ci[..., None] * br
    return ar, ai, bbr, bbi


def _complex_affine_combine(e1, e2):
    a1r, a1i, b1r, b1i = e1
    a2r, a2i, b2r, b2i = e2
    return (a2r * a1r - a2i * a1i,
            a2r * a1i + a2i * a1r,
            a2r * b1r - a2i * b1i + b2r,
            a2r * b1i + a2i * b1r + b2i)


def _ssm_scan(u, ar, ai, bbr, bbi, s0, reverse):
    bur = jnp.einsum('blgh,gph->blgp', u, bbr)
    bui = jnp.einsum('blgh,gph->blgp', u, bbi)
    a_r = jnp.broadcast_to(ar, bur.shape)
    a_i = jnp.broadcast_to(ai, bur.shape)
    acc_r, acc_i, sr, si = lax.associative_scan(_complex_affine_combine, (a_r, a_i, bur, bui), reverse=reverse, axis=1)
    if s0 is not None:
        s0r = s0[0][:, None]
        s0i = s0[1][:, None]
        sr = sr + acc_r * s0r - acc_i * s0i
        si = si + acc_r * s0i + acc_i * s0r
    return sr, si


def _readout(sr, si, c_re, c_im):
    return (jnp.einsum('blgp,ghp->blgh', sr, c_re.astype(jnp.float32))
            - jnp.einsum('blgp,ghp->blgh', si, c_im.astype(jnp.float32)))


def _s5_mixer(h, hc, lam_re, lam_im, log_step, b_re, b_im, c_re, c_im, d_skip, w_a, w_b, need_ctx):
    b, s, dm = h.shape
    lc = hc.shape[1]
    u = h.astype(jnp.float32).reshape(b, s, SSM_GROUPS, SSM_GROUP)
    uc = hc.astype(jnp.float32).reshape(b, lc, SSM_GROUPS, SSM_GROUP)
    y = jnp.zeros(u.shape, jnp.float32)
    yc = jnp.zeros(uc.shape, jnp.float32)
    for dr in range(2):
        rev = dr == 1
        ar, ai, bbr, bbi = _zoh(lam_re[dr], lam_im[dr], log_step[dr], b_re[dr], b_im[dr])
        scr, sci = _ssm_scan(uc, ar, ai, bbr, bbi, None, rev)
        last = 0 if rev else lc - 1
        sr, si = _ssm_scan(u, ar, ai, bbr, bbi, (scr[:, last], sci[:, last]), rev)
        y = y + _readout(sr, si, c_re[dr], c_im[dr])
        if need_ctx:
            yc = yc + _readout(scr, sci, c_re[dr], c_im[dr])

    def finish(yy, hh, n):
        z = yy.reshape(b, n, dm) + d_skip.astype(jnp.float32) * hh.astype(jnp.float32)
        z = jax.nn.gelu(z)
        out = (z @ w_a.astype(jnp.float32)) * jax.nn.sigmoid(z @ w_b.astype(jnp.float32))
        return out.astype(h.dtype)

    y_out = finish(y, h, s)
    yc_out = finish(yc, hc, lc) if need_ctx else None
    return y_out, yc_out


def _mlp(h, w1, w2):
    return jnp.square(jax.nn.relu(h @ w1)) @ w2


def setup_inputs(seed: int = 0) -> dict:
    key = jax.random.key(seed)
    ks = jax.random.split(key, 32)
    f32 = jnp.float32
    d = D_MODEL

    def nrm(k, shape, std):
        return jax.random.normal(k, shape, f32) * std

    return {
        'x': nrm(ks[0], (BATCH, SEQ, d), 1.0),
        'c': nrm(ks[1], (BATCH, d), 1.0),
        'ctx': nrm(ks[2], (BATCH, CTX_LEN, d), 1.0),
        'c_ctx': nrm(ks[3], (d,), 1.0),
        'w_ada': nrm(ks[4], (DEPTH, d, 6 * d), d ** -0.5),
        'b_ada': nrm(ks[5], (DEPTH, 6 * d), 0.02),
        'norm1_g': 1.0 + nrm(ks[6], (DEPTH, d), 0.02),
        'norm2_g': 1.0 + nrm(ks[7], (DEPTH, d), 0.02),
        'final_g': 1.0 + nrm(ks[8], (d,), 0.02),
        'w_in': nrm(ks[9], (N_EVEN, d, IN_PROJ), d ** -0.5),
        'w_out': nrm(ks[10], (N_EVEN, MIX_OUT, d), MIX_OUT ** -0.5),
        'lam_q1': nrm(ks[11], (N_EVEN, HEAD_DIM), 0.1),
        'lam_k1': nrm(ks[12], (N_EVEN, HEAD_DIM), 0.1),
        'lam_q2': nrm(ks[13], (N_EVEN, HEAD_DIM), 0.1),
        'lam_k2': nrm(ks[14], (N_EVEN, HEAD_DIM), 0.1),
        'subln_g': 1.0 + nrm(ks[15], (N_EVEN, DIFF_V_DIM), 0.02),
        'na_rpb': nrm(ks[16], (N_EVEN, NA_HEADS, 2 * NA_KH - 1, 2 * NA_KW - 1), 0.02),
        'ssm_lam_re': -0.5 + nrm(ks[17], (N_ODD, 2, SSM_GROUPS, SSM_STATE), 0.01),
        'ssm_lam_im': (math.pi * jnp.arange(SSM_STATE, dtype=f32))[None, None, None, :]
                      + nrm(ks[18], (N_ODD, 2, SSM_GROUPS, SSM_STATE), 0.01),
        'ssm_log_step': jax.random.uniform(ks[19], (N_ODD, 2, SSM_GROUPS), f32,
                                           minval=math.log(1e-3), maxval=math.log(1e-1)),
        'ssm_b_re': nrm(ks[20], (N_ODD, 2, SSM_GROUPS, SSM_STATE, SSM_GROUP), (2 * SSM_GROUP) ** -0.5),
        'ssm_b_im': nrm(ks[21], (N_ODD, 2, SSM_GROUPS, SSM_STATE, SSM_GROUP), (2 * SSM_GROUP) ** -0.5),
        'ssm_c_re': nrm(ks[22], (N_ODD, 2, SSM_GROUPS, SSM_GROUP, SSM_STATE), (2 * SSM_STATE) ** -0.5),
        'ssm_c_im': nrm(ks[23], (N_ODD, 2, SSM_GROUPS, SSM_GROUP, SSM_STATE), (2 * SSM_STATE) ** -0.5),
        'ssm_d': nrm(ks[24], (N_ODD, d), 1.0),
        'glu_w_a': nrm(ks[25], (N_ODD, d, d), d ** -0.5),
        'glu_w_b': nrm(ks[26], (N_ODD, d, d), d ** -0.5),
        'mlp_w1': nrm(ks[27], (DEPTH, d, MLP_HIDDEN), d ** -0.5),
        'mlp_w2': nrm(ks[28], (DEPTH, MLP_HIDDEN, d), MLP_HIDDEN ** -0.5),
    }


def reference(x, c, ctx, c_ctx, w_ada, b_ada, norm1_g, norm2_g, final_g, w_in, w_out,
              lam_q1, lam_k1, lam_q2, lam_k2, subln_g, na_rpb, ssm_lam_re, ssm_lam_im,
              ssm_log_step, ssm_b_re, ssm_b_im, ssm_c_re, ssm_c_im, ssm_d, glu_w_a, glu_w_b,
              mlp_w1, mlp_w2):
    s = x.shape[1]
    rows = s // GRID_W
    rope = _rope_tables(s)
    act_lat = jax.nn.silu(c)
    act_ctx = jax.nn.silu(c_ctx)
    for i in range(DEPTH):
        need_ctx = i < DEPTH - 1
        m = act_lat @ w_ada[i] + b_ada[i]
        mc = act_ctx @ w_ada[i] + b_ada[i]
        sh1, sc1, g1, sh2, sc2, g2 = jnp.split(m[:, None, :], 6, axis=-1)
        sh1c, sc1c, g1c, sh2c, sc2c, g2c = jnp.split(mc, 6, axis=-1)
        h = _rmsnorm(x, norm1_g[i], NORM_EPS) * (1.0 + sc1) + sh1
        hc = _rmsnorm(ctx, norm1_g[i], NORM_EPS) * (1.0 + sc1c) + sh1c
        if i % 2 == 0:
            e = i // 2
            lam_init = 0.8 - 0.6 * math.exp(-0.3 * i)
            y, yc = _even_mixer(h, hc, w_in[e], w_out[e], lam_q1[e], lam_k1[e], lam_q2[e], lam_k2[e],
                                subln_g[e], na_rpb[e], rope, lam_init, rows, need_ctx)
        else:
            o = i // 2
            y, yc = _s5_mixer(h, hc, ssm_lam_re[o], ssm_lam_im[o], ssm_log_step[o], ssm_b_re[o], ssm_b_im[o],
                              ssm_c_re[o], ssm_c_im[o], ssm_d[o], glu_w_a[o], glu_w_b[o], need_ctx)
        x = x + g1 * y
        h2 = _rmsnorm(x, norm2_g[i], NORM_EPS) * (1.0 + sc2) + sh2
        x = x + g2 * _mlp(h2, mlp_w1[i], mlp_w2[i])
        if need_ctx:
            ctx = ctx + g1c * yc
            h2c = _rmsnorm(ctx, norm2_g[i], NORM_EPS) * (1.0 + sc2c) + sh2c
            ctx = ctx + g2c * _mlp(h2c, mlp_w1[i], mlp_w2[i])
    return _rmsnorm(x, final_g, NORM_EPS)
```

```python
import functools
import math

import jax
import jax.numpy as jnp
from jax import lax
from jax.experimental import pallas as pl
from jax.experimental.pallas import tpu as pltpu

D_MODEL = 1024
DEPTH = 2
GRID_W = 64
HEAD_DIM = 64
DIFF_HEADS = 4
NA_HEADS = 8
NA_KH = 8
NA_KW = 16
ROPE_BASE = 10000.0
ROPE_AXIS_DIM = HEAD_DIM // 2
SSM_GROUP = 16
SSM_GROUPS = D_MODEL // SSM_GROUP
SSM_STATE = 64
MLP_HIDDEN = 4 * D_MODEL
A_QK = DIFF_HEADS * 2 * HEAD_DIM
A_V = DIFF_HEADS * 2 * HEAD_DIM
B_QKV = NA_HEADS * HEAD_DIM
IN_PROJ = 2 * A_QK + A_V + 3 * B_QKV
NORM_EPS = 1e-6
SUBLN_EPS = 1e-5

LANES = 128
SUBLANES = 8
SSM_CHUNK = 16
SSM_CW = SSM_CHUNK * SSM_GROUP
NEG_BIG = -1e30
VMEM_LIMIT = 56 * 1024 * 1024

BF = jnp.bfloat16
F32 = jnp.float32


def _cparams(sem):
    return pltpu.CompilerParams(dimension_semantics=sem, vmem_limit_bytes=VMEM_LIMIT)


def _dot(a, b):
    return jnp.dot(a, b, preferred_element_type=F32)


def _dot_nt(a, b):
    return lax.dot_general(a, b, (((1,), (1,)), ((), ())), preferred_element_type=F32)


def _rms(x, eps):
    return x * lax.rsqrt(jnp.mean(x * x, axis=-1, keepdims=True) + eps)


def _ada_kernel(a_ref, w_ref, b_ref, o_ref):
    a = a_ref[...]
    act = a * jax.nn.sigmoid(a)
    o_ref[0] = _dot(act.astype(BF), w_ref[0].astype(BF)) + b_ref[0]


def _ada(act_in, w_ada, b_ada):
    r, d = act_in.shape
    n = w_ada.shape[-1]
    tn = 1536
    return pl.pallas_call(
        _ada_kernel,
        grid=(DEPTH, n // tn),
        in_specs=[
            pl.BlockSpec((r, d), lambda i, j: (0, 0)),
            pl.BlockSpec((1, d, tn), lambda i, j: (i, 0, j)),
            pl.BlockSpec((1, 1, tn), lambda i, j: (i, 0, j)),
        ],
        out_specs=pl.BlockSpec((1, r, tn), lambda i, j: (i, 0, j)),
        out_shape=jax.ShapeDtypeStruct((DEPTH, r, n), F32),
        compiler_params=_cparams(("arbitrary", "arbitrary")),
        name="ada_ln",
    )(act_in, w_ada, b_ada.reshape(DEPTH, 1, n))


def _modulated_norm(x, g, sc, sh):
    return (_rms(x, NORM_EPS) * g) * (1.0 + sc) + sh


def _rope_chunk(x, cos, sin, low):
    up = pltpu.roll(x, LANES - 16, 1)
    dn = pltpu.roll(x, 16, 1)
    return x * cos + jnp.where(low, up, dn) * sin


def _inproj_kernel(x_ref, g_ref, sc_ref, sh_ref, w_ref, cos_ref, sin_ref, o_ref, *, rope):
    h = _modulated_norm(x_ref[0], g_ref[...], sc_ref[0], sh_ref[0]).astype(BF)
    tm = h.shape[0]
    if rope:
        cos = cos_ref[...]
        sin = sin_ref[...]
        low = (lax.broadcasted_iota(jnp.int32, (tm, LANES), 1) % 32) < 16
    cw = 512
    for j in range(IN_PROJ // cw):
        acc = _dot(h, w_ref[:, j * cw:(j + 1) * cw])
        if j in (0, 3):
            acc = acc * (HEAD_DIM ** -0.5)
        if rope and j < 2:
            parts = [_rope_chunk(acc[:, c * LANES:(c + 1) * LANES], cos, sin, low) for c in range(cw // LANES)]
            acc = jnp.concatenate(parts, axis=-1)
        o_ref[0, :, j * cw:(j + 1) * cw] = acc.astype(BF)


def _inproj(x, g, sc, sh, w, cos_t, sin_t, *, rope, tm):
    b, s, d = x.shape
    n = w.shape[-1]
    mod = pl.BlockSpec((1, 1, d), lambda i, t: (i, 0, 0))
    return pl.pallas_call(
        functools.partial(_inproj_kernel, rope=rope),
        grid=(b, s // tm),
        in_specs=[
            pl.BlockSpec((1, tm, d), lambda i, t: (i, t, 0)),
            pl.BlockSpec((1, d), lambda i, t: (0, 0)),
            mod, mod,
            pl.BlockSpec((d, n), lambda i, t: (0, 0)),
            pl.BlockSpec((tm, LANES), lambda i, t: (t, 0)),
            pl.BlockSpec((tm, LANES), lambda i, t: (t, 0)),
        ],
        out_specs=pl.BlockSpec((1, tm, n), lambda i, t: (i, t, 0)),
        out_shape=jax.ShapeDtypeStruct((b, s, n), BF),
        compiler_params=_cparams(("arbitrary", "arbitrary")),
        name="in_proj_rope" if rope else "in_proj_ctx",
    )(x, g, sc, sh, w, cos_t, sin_t)


def _rope_tables(seq):
    t = jnp.arange(seq)
    row = (t // GRID_W).astype(F32)
    col = (t % GRID_W).astype(F32)
    n_freq = ROPE_AXIS_DIM // 2
    inv_freq = ROPE_BASE ** (-jnp.arange(n_freq, dtype=F32) / n_freq)
    ang_r = row[:, None] * inv_freq[None, :]
    ang_c = col[:, None] * inv_freq[None, :]
    cos64 = jnp.concatenate([jnp.cos(ang_r), jnp.cos(ang_r), jnp.cos(ang_c), jnp.cos(ang_c)], axis=-1)
    sin64 = jnp.concatenate([-jnp.sin(ang_r), jnp.sin(ang_r), -jnp.sin(ang_c), jnp.sin(ang_c)], axis=-1)
    return jnp.tile(cos64, (1, 2)), jnp.tile(sin64, (1, 2))


def _attend(qm, ks, vs, biases):
    scores = [_dot_nt(qm, k) for k in ks]
    scores = [s if b is None else s + b for s, b in zip(scores, biases)]
    mx = functools.reduce(jnp.maximum, [jnp.max(s, axis=-1, keepdims=True) for s in scores])
    ps = [jnp.exp(s - mx) for s in scores]
    den = functools.reduce(jnp.add, [jnp.sum(p, axis=-1, keepdims=True) for p in ps])
    num = functools.reduce(jnp.add, [_dot(p.astype(BF), v) for p, v in zip(ps, vs)])
    return num / den


def _diff_lambda(lam_ref, lam_init):
    lp = lam_ref[...]
    s1 = jnp.sum(lp[0:1] * lp[1:2], axis=-1, keepdims=True)
    s2 = jnp.sum(lp[2:3] * lp[3:4], axis=-1, keepdims=True)
    return jnp.exp(s1) - jnp.exp(s2) + lam_init


def _diff_head(q, ks, vs, lam, gsub, lam_init):
    lane = lax.broadcasted_iota(jnp.int32, q.shape, 1)
    zero = jnp.zeros_like(q)
    o0 = _attend(jnp.where(lane < HEAD_DIM, q, zero), ks, vs, [None] * len(ks))
    o1 = _attend(jnp.where(lane >= HEAD_DIM, q, zero), ks, vs, [None] * len(ks))
    o = o0 - lam * o1
    return (_rms(o, SUBLN_EPS) * gsub) * (1.0 - lam_init)


def _diff_kernel(lam_ref, gs_ref, q_ref, *refs, n_kv, lam_init):
    k_refs, v_refs, o_ref = refs[:n_kv], refs[n_kv:2 * n_kv], refs[2 * n_kv]
    lam = _diff_lambda(lam_ref, lam_init)
    o = _diff_head(q_ref[0], [r[0] for r in k_refs], [r[0] for r in v_refs], lam, gs_ref[...], lam_init)
    o_ref[0] = o.astype(BF)


def _diff_attention(lam_p, gsub, q_src, kv_srcs, *, tq, lam_init, name):
    b, sq, _ = q_src.shape
    hw = 2 * HEAD_DIM
    k0, v0 = A_QK // hw, 2 * A_QK // hw
    specs = [
        pl.BlockSpec(lam_p.shape, lambda i, h, t: (0, 0)),
        pl.BlockSpec((1, hw), lambda i, h, t: (0, 0)),
        pl.BlockSpec((1, tq, hw), lambda i, h, t: (i, t, h)),
    ]
    specs += [pl.BlockSpec((1, s.shape[1], hw), lambda i, h, t: (i, 0, k0 + h)) for s in kv_srcs]
    specs += [pl.BlockSpec((1, s.shape[1], hw), lambda i, h, t: (i, 0, v0 + h)) for s in kv_srcs]
    return pl.pallas_call(
        functools.partial(_diff_kernel, n_kv=len(kv_srcs), lam_init=lam_init),
        grid=(b, DIFF_HEADS, sq // tq),
        in_specs=specs,
        out_specs=pl.BlockSpec((1, tq, hw), lambda i, h, t: (i, t, h)),
        out_shape=jax.ShapeDtypeStruct((b, sq, A_V), BF),
        compiler_params=_cparams(("arbitrary", "arbitrary", "arbitrary")),
        name=name,
    )(lam_p, gsub, q_src, *kv_srcs, *kv_srcs)


def _na_kernel(q_ref, kl_ref, vl_ref, kc_ref, vc_ref, bias_ref, o_ref, *, rows):
    r = pl.program_id(1)
    r_start = jnp.clip(r - NA_KH // 2, 0, rows - NA_KH)
    start = pl.multiple_of(r_start * GRID_W, GRID_W)
    band = NA_KH * GRID_W
    lane = lax.broadcasted_iota(jnp.int32, (GRID_W, LANES), 1)
    for hp in range(NA_HEADS // 2):
        cs = slice(hp * LANES, (hp + 1) * LANES)
        q2 = q_ref[0, :, cs]
        ks = [kl_ref[0, pl.ds(start, band), cs], kc_ref[0, :, cs]]
        vs = [vl_ref[0, pl.ds(start, band), cs], vc_ref[0, :, cs]]
        zero = jnp.zeros_like(q2)
        o_lo = _attend(jnp.where(lane < HEAD_DIM, q2, zero), ks, vs, [bias_ref[2 * hp, 0], None])
        o_hi = _attend(jnp.where(lane >= HEAD_DIM, q2, zero), ks, vs, [bias_ref[2 * hp + 1, 0], None])
        o_ref[0, :, cs] = jnp.where(lane < HEAD_DIM, o_lo, o_hi).astype(BF)


def _na_bias_case(r, rows):
    lo = NA_KH // 2
    hi = rows - NA_KH // 2
    return jnp.where(r < lo, r, jnp.where(r <= hi, lo, r - hi + lo))


def _na_bias_table(rpb, rows):
    kh = NA_KH
    lo = kh // 2
    hi = rows - kh // 2
    case_rows = jnp.concatenate([jnp.arange(lo), jnp.array([lo]), jnp.arange(hi + 1, rows)])
    r_start = jnp.clip(case_rows - kh // 2, 0, rows - kh)
    row_off = r_start[:, None] + jnp.arange(kh)[None, :] - case_rows[:, None] + (NA_KH - 1)
    cq = jnp.arange(GRID_W)
    ck = jnp.arange(GRID_W)
    col_start = jnp.clip(cq - NA_KW // 2, 0, GRID_W - NA_KW)
    in_win = (ck[None, :] >= col_start[:, None]) & (ck[None, :] < col_start[:, None] + NA_KW)
    col_off = jnp.clip(ck[None, :] - cq[:, None] + (NA_KW - 1), 0, 2 * NA_KW - 2)
    t = rpb[:, row_off]
    t = t[:, :, :, col_off]
    t = jnp.where(in_win[None, None, None], t.astype(F32), NEG_BIG)
    t = t.transpose(0, 1, 3, 2, 4)
    return t.reshape(NA_HEADS, case_rows.shape[0], GRID_W, kh * GRID_W)


def _na_attention(p_lat, p_ctx, bias_tab, rows):
    b, s, _ = p_lat.shape
    lc = p_ctx.shape[1]
    base = (2 * A_QK + A_V) // B_QKV
    band = NA_KH * GRID_W
    return pl.pallas_call(
        functools.partial(_na_kernel, rows=rows),
        grid=(b, rows),
        in_specs=[
            pl.BlockSpec((1, GRID_W, B_QKV), lambda i, r: (i, r, base)),
            pl.BlockSpec((1, s, B_QKV), lambda i, r: (i, 0, base + 1)),
            pl.BlockSpec((1, s, B_QKV), lambda i, r: (i, 0, base + 2)),
            pl.BlockSpec((1, lc, B_QKV), lambda i, r: (i, 0, base + 1)),
            pl.BlockSpec((1, lc, B_QKV), lambda i, r: (i, 0, base + 2)),
            pl.BlockSpec((NA_HEADS, 1, GRID_W, band), lambda i, r: (0, _na_bias_case(r, rows), 0, 0)),
        ],
        out_specs=pl.BlockSpec((1, GRID_W, B_QKV), lambda i, r: (i, r, 0)),
        out_shape=jax.ShapeDtypeStruct((b, s, B_QKV), BF),
        compiler_params=_cparams(("arbitrary", "arbitrary")),
        name="na_attn",
    )(p_lat, p_lat, p_lat, p_ctx, p_ctx, bias_tab)


def _dense_ctx_kernel(q_ref, k_ref, v_ref, o_ref):
    lane = lax.broadcasted_iota(jnp.int32, (q_ref.shape[1], LANES), 1)
    for hp in range(NA_HEADS // 2):
        cs = slice(hp * LANES, (hp + 1) * LANES)
        q2 = q_ref[0, :, cs]
        ks, vs = [k_ref[0, :, cs]], [v_ref[0, :, cs]]
        zero = jnp.zeros_like(q2)
        o_lo = _attend(jnp.where(lane < HEAD_DIM, q2, zero), ks, vs, [None])
        o_hi = _attend(jnp.where(lane >= HEAD_DIM, q2, zero), ks, vs, [None])
        o_ref[0, :, cs] = jnp.where(lane < HEAD_DIM, o_lo, o_hi).astype(BF)


def _dense_ctx_attention(p_ctx):
    b, lc, _ = p_ctx.shape
    base = (2 * A_QK + A_V) // B_QKV
    spec = lambda j: pl.BlockSpec((1, lc, B_QKV), lambda i: (i, 0, base + j))
    return pl.pallas_call(
        _dense_ctx_kernel,
        grid=(b,),
        in_specs=[spec(0), spec(1), spec(2)],
        out_specs=pl.BlockSpec((1, lc, B_QKV), lambda i: (i, 0, 0)),
        out_shape=jax.ShapeDtypeStruct((b, lc, B_QKV), BF),
        compiler_params=_cparams(("arbitrary",)),
        name="dense_ctx_attn",
    )(p_ctx, p_ctx, p_ctx)


def _mlp_residual(x1, n2g, sc2, sh2, g2, w1_ref, w2_ref):
    h2 = _modulated_norm(x1, n2g, sc2, sh2).astype(BF)
    hc = 1024
    acc = jnp.zeros_like(x1)
    for c in range(MLP_HIDDEN // hc):
        hid = jnp.maximum(_dot(h2, w1_ref[:, c * hc:(c + 1) * hc]), 0.0)
        acc = acc + _dot((hid * hid).astype(BF), w2_ref[c * hc:(c + 1) * hc, :])
    return x1 + g2 * acc


def _even_tail_kernel(oa_ref, ob_ref, x_ref, wo_ref, w1_ref, w2_ref, n2g_ref, g1_ref, sc2_ref, sh2_ref, g2_ref,
                      n1g_ref, sc1n_ref, sh1n_ref, xo_ref, hn_ref):
    y = _dot(oa_ref[0], wo_ref[0:A_V, :]) + _dot(ob_ref[0], wo_ref[A_V:, :])
    x1 = x_ref[0] + g1_ref[0] * y
    x2 = _mlp_residual(x1, n2g_ref[...], sc2_ref[0], sh2_ref[0], g2_ref[0], w1_ref, w2_ref)
    xo_ref[0] = x2
    hn_ref[0] = _modulated_norm(x2, n1g_ref[...], sc1n_ref[0], sh1n_ref[0])


def _even_tail(oa, ob, x, wo, w1, w2, n2g, g1, sc2, sh2, g2, n1g, sc1n, sh1n, *, tm):
    b, s, d = x.shape
    tok = lambda w: pl.BlockSpec((1, tm, w), lambda i, t: (i, t, 0))
    mod = pl.BlockSpec((1, 1, d), lambda i, t: (i, 0, 0))
    vec = pl.BlockSpec((1, d), lambda i, t: (0, 0))
    full = lambda a: pl.BlockSpec(a.shape, lambda i, t: (0, 0))
    return pl.pallas_call(
        _even_tail_kernel,
        grid=(b, s // tm),
        in_specs=[tok(A_V), tok(B_QKV), tok(d), full(wo), full(w1), full(w2), vec, mod, mod, mod, mod, vec, mod, mod],
        out_specs=[tok(d), tok(d)],
        out_shape=[jax.ShapeDtypeStruct((b, s, d), F32), jax.ShapeDtypeStruct((b, s, d), F32)],
        compiler_params=_cparams(("arbitrary", "arbitrary")),
        name="even_tail",
    )(oa, ob, x, wo, w1, w2, n2g, g1, sc2, sh2, g2, n1g, sc1n, sh1n)


def _odd_tail_kernel(y_ref, h_ref, x_ref, dsk_ref, wa_ref, wb_ref, w1_ref, w2_ref, n2g_ref, g1_ref, sc2_ref, sh2_ref,
                     g2_ref, fg_ref, o_ref):
    z = y_ref[0] + dsk_ref[...] * h_ref[0]
    z = jax.nn.gelu(z).astype(BF)
    mix = _dot(z, wa_ref[...]) * jax.nn.sigmoid(_dot(z, wb_ref[...]))
    x1 = x_ref[0] + g1_ref[0] * mix
    x2 = _mlp_residual(x1, n2g_ref[...], sc2_ref[0], sh2_ref[0], g2_ref[0], w1_ref, w2_ref)
    o_ref[0] = _rms(x2, NORM_EPS) * fg_ref[...]


def _odd_tail(y, h, x, dsk, wa, wb, w1, w2, n2g, g1, sc2, sh2, g2, fg, *, tm):
    b, s, d = x.shape
    tok = pl.BlockSpec((1, tm, d), lambda i, t: (i, t, 0))
    mod = pl.BlockSpec((1, 1, d), lambda i, t: (i, 0, 0))
    vec = pl.BlockSpec((1, d), lambda i, t: (0, 0))
    full = lambda a: pl.BlockSpec(a.shape, lambda i, t: (0, 0))
    return pl.pallas_call(
        _odd_tail_kernel,
        grid=(b, s // tm),
        in_specs=[tok, tok, tok, vec, full(wa), full(wb), full(w1), full(w2), vec, mod, mod, mod, mod, vec],
        out_specs=tok,
        out_shape=jax.ShapeDtypeStruct((b, s, d), F32),
        compiler_params=_cparams(("arbitrary", "arbitrary")),
        name="odd_tail",
    )(y, h, x, dsk, wa, wb, w1, w2, n2g, g1, sc2, sh2, g2, fg)


def _s5_weights(lam_re, lam_im, log_step, b_re, b_im, c_re, c_im):
    t_len = SSM_CHUNK
    hp = lax.Precision.HIGHEST
    lr = lam_re.astype(F32)
    li = lam_im.astype(F32)
    delta = jnp.exp(log_step.astype(F32))[..., None]
    k = jnp.arange(t_len + 1, dtype=F32)[None, None, :, None]
    mag = jnp.exp((lr * delta)[:, :, None, :] * k)
    ang = (li * delta)[:, :, None, :] * k
    ar_k = mag * jnp.cos(ang)
    ai_k = mag * jnp.sin(ang)
    ar1, ai1 = ar_k[:, :, 1], ai_k[:, :, 1]
    den = lr * lr + li * li
    nr = ar1 - 1.0
    cr = (nr * lr + ai1 * li) / den
    ci = (ai1 * lr - nr * li) / den
    br = b_re.astype(F32)
    bi = b_im.astype(F32)
    bbr = cr[..., None] * br - ci[..., None] * bi
    bbi = cr[..., None] * bi + ci[..., None] * br
    cre = c_re.astype(F32)[:, :, None]
    cim = c_im.astype(F32)[:, :, None]
    car = cre * ar_k[:, :, :, None, :] - cim * ai_k[:, :, :, None, :]
    cai = cre * ai_k[:, :, :, None, :] + cim * ar_k[:, :, :, None, :]
    kern = (jnp.einsum('dgkop,dgpi->dgkoi', car[:, :, :t_len], bbr, precision=hp)
            - jnp.einsum('dgkop,dgpi->dgkoi', cai[:, :, :t_len], bbi, precision=hp))
    abr = ar_k[:, :, :t_len, :, None] * bbr[:, :, None] - ai_k[:, :, :t_len, :, None] * bbi[:, :, None]
    abi = ar_k[:, :, :t_len, :, None] * bbi[:, :, None] + ai_k[:, :, :t_len, :, None] * bbr[:, :, None]

    g = lr.shape[1]
    s_idx = jnp.arange(t_len)[:, None]
    t_idx = jnp.arange(t_len)[None, :]

    def toeplitz(kd, lag, valid):
        m = kd[:, jnp.clip(lag, 0, t_len - 1)]
        m = jnp.where(valid[None, :, :, None, None], m, 0.0)
        return m.transpose(0, 1, 4, 2, 3).reshape(g, SSM_CW, SSM_CW)

    toep = toeplitz(kern[0], t_idx - s_idx, t_idx >= s_idx) + toeplitz(kern[1], s_idx - t_idx, s_idx >= t_idx)

    def in_op(ab, flip):
        ab = ab[:, ::-1] if flip else ab
        return ab.transpose(0, 1, 3, 2).reshape(g, SSM_CW, SSM_STATE)

    w_in = jnp.stack([in_op(abr[0], True), in_op(abi[0], True), in_op(abr[1], False), in_op(abi[1], False)], axis=2)

    def st_op(ca, lo, flip):
        ca = ca[:, lo:lo + t_len]
        ca = ca[:, ::-1] if flip else ca
        return ca.transpose(0, 3, 1, 2).reshape(g, SSM_STATE, SSM_CW)

    w_st = jnp.stack([st_op(car[0], 1, False), -st_op(cai[0], 1, False),
                      st_op(car[1], 1, True), -st_op(cai[1], 1, True)], axis=1)
    a_t = jnp.stack([ar_k[0, :, t_len], ai_k[0, :, t_len], ar_k[1, :, t_len], ai_k[1, :, t_len]], axis=1)

    eye = jnp.eye(2, dtype=F32)
    w_in_p = w_in.reshape(g // 2, 2, SSM_CW, 4, 1, SSM_STATE) * eye[None, :, None, None, :, None]
    w_in_p = w_in_p.reshape(g, SSM_CW, 8 * SSM_STATE)
    w_st_p = w_st.reshape(g // 2, 2, 4, 1, SSM_STATE, SSM_CW) * eye[None, :, None, :, None, None]
    w_st_p = w_st_p.reshape(g, 8 * SSM_STATE, SSM_CW)
    a_t_p = a_t.reshape(g // 2, 2, 4, SSM_STATE).transpose(0, 2, 1, 3).reshape(g // 2, 4, 2 * SSM_STATE)
    return toep.astype(BF), w_in_p.astype(BF), w_st_p.astype(BF), a_t_p


def _s5_kernel(ul_ref, uc_ref, toep_ref, win_ref, wst_ref, at_ref, y_ref, sloc_ref, sin_ref, *, n_ctx, n_lat, nb):
    w = 2 * SSM_STATE
    lat0 = n_ctx * nb
    sloc_ref[0:lat0, :] = _dot(uc_ref[0], win_ref[0]) + _dot(uc_ref[1], win_ref[1])
    sloc_ref[lat0:, :] = _dot(ul_ref[0], win_ref[0]) + _dot(ul_ref[1], win_ref[1])

    afr = jnp.broadcast_to(at_ref[0, 0:1, :], (nb, w))
    afi = jnp.broadcast_to(at_ref[0, 1:2, :], (nb, w))
    abr = jnp.broadcast_to(at_ref[0, 2:3, :], (nb, w))
    abi = jnp.broadcast_to(at_ref[0, 3:4, :], (nb, w))

    def advance(carry, rf, rb):
        fr, fi, br, bi = carry
        xfr = sloc_ref[pl.ds(rf, nb), 0:w]
        xfi = sloc_ref[pl.ds(rf, nb), w:2 * w]
        xbr = sloc_ref[pl.ds(rb, nb), 2 * w:3 * w]
        xbi = sloc_ref[pl.ds(rb, nb), 3 * w:4 * w]
        return (afr * fr - afi * fi + xfr, afr * fi + afi * fr + xfi,
                abr * br - abi * bi + xbr, abr * bi + abi * br + xbi)

    def ctx_body(i, carry):
        rf = pl.multiple_of(i * nb, nb)
        rb = pl.multiple_of((n_ctx - 1 - i) * nb, nb)
        return advance(carry, rf, rb)

    def lat_body(i, carry):
        fr, fi, br, bi = carry
        rf = pl.multiple_of(i * nb, nb)
        rb = pl.multiple_of((n_lat - 1 - i) * nb, nb)
        sin_ref[pl.ds(rf, nb), 0:w] = fr
        sin_ref[pl.ds(rf, nb), w:2 * w] = fi
        sin_ref[pl.ds(rb, nb), 2 * w:3 * w] = br
        sin_ref[pl.ds(rb, nb), 3 * w:4 * w] = bi
        return advance(carry, lat0 + rf, lat0 + rb)

    zero = jnp.zeros((nb, w), F32)
    carry = lax.fori_loop(0, n_ctx, ctx_body, (zero, zero, zero, zero))
    lax.fori_loop(0, n_lat, lat_body, carry)

    s_in = sin_ref[...].astype(BF)
    for j in range(2):
        y_ref[j] = _dot(ul_ref[j], toep_ref[j]) + _dot(s_in, wst_ref[j])


def _s5_scan(u_lat, u_ctx, toep, w_in, w_st, a_t, *, nb):
    g, rl, cw = u_lat.shape
    rc = u_ctx.shape[1]
    pair = lambda r, c: pl.BlockSpec((2, r, c), lambda i: (i, 0, 0))
    return pl.pallas_call(
        functools.partial(_s5_kernel, n_ctx=rc // nb, n_lat=rl // nb, nb=nb),
        grid=(g // 2,),
        in_specs=[pair(rl, cw), pair(rc, cw), pair(cw, cw), pair(cw, 8 * SSM_STATE), pair(8 * SSM_STATE, cw),
                  pl.BlockSpec((1, 4, 2 * SSM_STATE), lambda i: (i, 0, 0))],
        out_specs=pair(rl, cw),
        out_shape=jax.ShapeDtypeStruct((g, rl, cw), F32),
        scratch_shapes=[pltpu.VMEM((rl + rc, 8 * SSM_STATE), F32), pltpu.VMEM((rl, 8 * SSM_STATE), F32)],
        compiler_params=_cparams(("arbitrary",)),
        name="s5_scan",
    )(u_lat, u_ctx, toep, w_in, w_st, a_t)


def _to_group_major(h):
    b, n, _ = h.shape
    hb = h.astype(BF).reshape(b, n // SSM_CHUNK, SSM_CHUNK, SSM_GROUPS, SSM_GROUP)
    return hb.transpose(3, 1, 0, 2, 4).reshape(SSM_GROUPS, (n // SSM_CHUNK) * b, SSM_CW)


def _from_group_major(y, b):
    g, r, _ = y.shape
    n_chunk = r // b
    yy = y.reshape(g, n_chunk, b, SSM_CHUNK, SSM_GROUP).transpose(2, 1, 3, 0, 4)
    return yy.reshape(b, n_chunk * SSM_CHUNK, g * SSM_GROUP)


def kernel(x, c, ctx, c_ctx, w_ada, b_ada, norm1_g, norm2_g, final_g, w_in, w_out, lam_q1, lam_k1, lam_q2, lam_k2,
           subln_g, na_rpb, ssm_lam_re, ssm_lam_im, ssm_log_step, ssm_b_re, ssm_b_im, ssm_c_re, ssm_c_im, ssm_d,
           glu_w_a, glu_w_b, mlp_w1, mlp_w2):
    b, s, d = x.shape
    lc = ctx.shape[1]
    rows = s // GRID_W
    assert DEPTH == 2 and d == D_MODEL and s % GRID_W == 0 and rows >= NA_KH
    assert s % SSM_CHUNK == 0 and lc % SSM_CHUNK == 0 and b % SUBLANES == 0

    n_act = -(-(b + 1) // SUBLANES) * SUBLANES
    act_in = jnp.zeros((n_act, d), F32).at[:b].set(c).at[b].set(c_ctx)
    m_all = _ada(act_in, w_ada, b_ada)

    def mods(i):
        lat = [m_all[i, :b, j * d:(j + 1) * d].reshape(b, 1, d) for j in range(6)]
        cx = [jnp.broadcast_to(m_all[i, b, j * d:(j + 1) * d].reshape(1, 1, d), (b, 1, d)) for j in range(6)]
        return lat, cx

    (sh1, sc1, g1, sh2, sc2, g2), (sh1c, sc1c, g1c, sh2c, sc2c, g2c) = mods(0)
    (sh1n, sc1n, g1n, sh2n, sc2n, g2n), (sh1nc, sc1nc, _, _, _, _) = mods(1)
    n1g = [norm1_g[i].reshape(1, d) for i in range(DEPTH)]
    n2g = [norm2_g[i].reshape(1, d) for i in range(DEPTH)]

    cos_t, sin_t = _rope_tables(s)
    w_in_b = w_in[0].astype(BF)
    p_lat = _inproj(x, n1g[0], sc1, sh1, w_in_b, cos_t, sin_t, rope=True, tm=1024)
    p_ctx = _inproj(ctx, n1g[0], sc1c, sh1c, w_in_b, cos_t[:lc], sin_t[:lc], rope=False, tm=lc)

    lam_init = 0.8 - 0.6 * math.exp(-0.3 * 0)
    lam_p = jnp.stack([lam_q1[0], lam_k1[0], lam_q2[0], lam_k2[0]]).astype(F32)
    gsub = subln_g[0].reshape(1, 2 * HEAD_DIM).astype(F32)
    oa = _diff_attention(lam_p, gsub, p_lat, [p_lat, p_ctx], tq=512, lam_init=lam_init, name="diff_attn")
    oac = _diff_attention(lam_p, gsub, p_ctx, [p_ctx], tq=lc, lam_init=lam_init, name="diff_attn_ctx")
    ob = _na_attention(p_lat, p_ctx, _na_bias_table(na_rpb[0], rows), rows)
    obc = _dense_ctx_attention(p_ctx)

    wo_b = w_out[0].astype(BF)
    w1_b = [mlp_w1[i].astype(BF) for i in range(DEPTH)]
    w2_b = [mlp_w2[i].astype(BF) for i in range(DEPTH)]
    x1, h1 = _even_tail(oa, ob, x, wo_b, w1_b[0], w2_b[0], n2g[0], g1, sc2, sh2, g2, n1g[1], sc1n, sh1n, tm=512)
    _, hc1 = _even_tail(oac, obc, ctx, wo_b, w1_b[0], w2_b[0], n2g[0], g1c, sc2c, sh2c, g2c, n1g[1], sc1nc, sh1nc,
                        tm=lc)

    toep, w_si, w_st, a_t = _s5_weights(ssm_lam_re[0], ssm_lam_im[0], ssm_log_step[0], ssm_b_re[0], ssm_b_im[0],
                                        ssm_c_re[0], ssm_c_im[0])
    y_g = _s5_scan(_to_group_major(h1), _to_group_major(hc1), toep, w_si, w_st, a_t, nb=b)
    y = _from_group_major(y_g, b)
    return _odd_tail(y, h1, x1, ssm_d[0].reshape(1, d).astype(F32), glu_w_a[0].astype(BF), glu_w_b[0].astype(BF),
                     w1_b[1], w2_b[1], n2g[1], g1n, sc2n, sh2n, g2n, final_g.reshape(1, d).astype(F32), tm=512)
```

```python
import functools
import math

import jax
import jax.numpy as jnp
import numpy as np
from jax import lax
from jax.experimental import pallas as pl
from jax.experimental.pallas import tpu as pltpu

D_MODEL = 1024
DEPTH = 2
GRID_W = 64
HEAD_DIM = 64
DIFF_HEADS = 4
NA_HEADS = 8
NA_KH = 8
NA_KW = 16
NA_QROWS = 4
NA_WIN_ROWS = 12
ROPE_BASE = 10000.0
ROPE_AXIS_DIM = HEAD_DIM // 2
SSM_GROUP = 16
SSM_GROUPS = D_MODEL // SSM_GROUP
SSM_STATE = 64
MLP_HIDDEN = 4 * D_MODEL
A_QK = DIFF_HEADS * 2 * HEAD_DIM
A_V = DIFF_HEADS * 2 * HEAD_DIM
B_QKV = NA_HEADS * HEAD_DIM
IN_PROJ = 2 * A_QK + A_V + 3 * B_QKV
NORM_EPS = 1e-6
SUBLN_EPS = 1e-5

LANES = 128
SUBLANES = 8
SSM_CHUNK = 16
SSM_CW = SSM_CHUNK * SSM_GROUP
NEG_BIG = -1e30
VMEM_LIMIT = 56 * 1024 * 1024

BF = jnp.bfloat16
F32 = jnp.float32


def _cparams(sem):
    return pltpu.CompilerParams(dimension_semantics=sem, vmem_limit_bytes=VMEM_LIMIT)


def _dot(a, b):
    return jnp.dot(a, b, preferred_element_type=F32)


def _dot_nt(a, b):
    return lax.dot_general(a, b, (((1,), (1,)), ((), ())), preferred_element_type=F32)


def _rms(x, eps):
    return x * lax.rsqrt(jnp.mean(x * x, axis=-1, keepdims=True) + eps)


def _ada_kernel(a_ref, w_ref, b_ref, o_ref):
    a = a_ref[...]
    act = a * jax.nn.sigmoid(a)
    o_ref[0] = _dot(act.astype(BF), w_ref[0].astype(BF)) + b_ref[0]


def _ada(act_in, w_ada, b_ada):
    r, d = act_in.shape
    n = w_ada.shape[-1]
    tn = 1536
    return pl.pallas_call(
        _ada_kernel,
        grid=(DEPTH, n // tn),
        in_specs=[
            pl.BlockSpec((r, d), lambda i, j: (0, 0)),
            pl.BlockSpec((1, d, tn), lambda i, j: (i, 0, j)),
            pl.BlockSpec((1, 1, tn), lambda i, j: (i, 0, j)),
        ],
        out_specs=pl.BlockSpec((1, r, tn), lambda i, j: (i, 0, j)),
        out_shape=jax.ShapeDtypeStruct((DEPTH, r, n), F32),
        compiler_params=_cparams(("arbitrary", "arbitrary")),
        name="ada_ln",
    )(act_in, w_ada, b_ada.reshape(DEPTH, 1, n))


def _modulated_norm(x, g, sc, sh):
    return (_rms(x, NORM_EPS) * g) * (1.0 + sc) + sh


def _rope_chunk(x, cos, sin, low):
    up = pltpu.roll(x, LANES - 16, 1)
    dn = pltpu.roll(x, 16, 1)
    return x * cos + jnp.where(low, up, dn) * sin


def _inproj_kernel(x_ref, g_ref, sc_ref, sh_ref, w_ref, cos_ref, sin_ref, o_ref, *, rope):
    h = _modulated_norm(x_ref[0], g_ref[...], sc_ref[0], sh_ref[0]).astype(BF)
    tm = h.shape[0]
    if rope:
        cos = cos_ref[...]
        sin = sin_ref[...]
        low = (lax.broadcasted_iota(jnp.int32, (tm, LANES), 1) % 32) < 16
    cw = 512
    for j in range(IN_PROJ // cw):
        acc = _dot(h, w_ref[:, j * cw:(j + 1) * cw])
        if j in (0, 3):
            acc = acc * (HEAD_DIM ** -0.5)
        if rope and j < 2:
            parts = [_rope_chunk(acc[:, c * LANES:(c + 1) * LANES], cos, sin, low) for c in range(cw // LANES)]
            acc = jnp.concatenate(parts, axis=-1)
        o_ref[0, :, j * cw:(j + 1) * cw] = acc.astype(BF)


def _inproj(x, g, sc, sh, w, cos_t, sin_t, *, rope, tm):
    b, s, d = x.shape
    n = w.shape[-1]
    mod = pl.BlockSpec((1, 1, d), lambda i, t: (i, 0, 0))
    return pl.pallas_call(
        functools.partial(_inproj_kernel, rope=rope),
        grid=(b, s // tm),
        in_specs=[
            pl.BlockSpec((1, tm, d), lambda i, t: (i, t, 0)),
            pl.BlockSpec((1, d), lambda i, t: (0, 0)),
            mod, mod,
            pl.BlockSpec((d, n), lambda i, t: (0, 0)),
            pl.BlockSpec((tm, LANES), lambda i, t: (t, 0)),
            pl.BlockSpec((tm, LANES), lambda i, t: (t, 0)),
        ],
        out_specs=pl.BlockSpec((1, tm, n), lambda i, t: (i, t, 0)),
        out_shape=jax.ShapeDtypeStruct((b, s, n), BF),
        compiler_params=_cparams(("arbitrary", "arbitrary")),
        name="in_proj_rope" if rope else "in_proj_ctx",
    )(x, g, sc, sh, w, cos_t, sin_t)


def _rope_tables(seq):
    t = jnp.arange(seq)
    row = (t // GRID_W).astype(F32)
    col = (t % GRID_W).astype(F32)
    n_freq = ROPE_AXIS_DIM // 2
    inv_freq = ROPE_BASE ** (-jnp.arange(n_freq, dtype=F32) / n_freq)
    ang_r = row[:, None] * inv_freq[None, :]
    ang_c = col[:, None] * inv_freq[None, :]
    cos64 = jnp.concatenate([jnp.cos(ang_r), jnp.cos(ang_r), jnp.cos(ang_c), jnp.cos(ang_c)], axis=-1)
    sin64 = jnp.concatenate([-jnp.sin(ang_r), jnp.sin(ang_r), -jnp.sin(ang_c), jnp.sin(ang_c)], axis=-1)
    return jnp.tile(cos64, (1, 2)), jnp.tile(sin64, (1, 2))


def _attend(qm, ks, vs, biases):
    scores = [_dot_nt(qm, k) for k in ks]
    scores = [s if b is None else s + b for s, b in zip(scores, biases)]
    mx = functools.reduce(jnp.maximum, [jnp.max(s, axis=-1, keepdims=True) for s in scores])
    ps = [jnp.exp(s - mx) for s in scores]
    den = functools.reduce(jnp.add, [jnp.sum(p, axis=-1, keepdims=True) for p in ps])
    num = functools.reduce(jnp.add, [_dot(p.astype(BF), v) for p, v in zip(ps, vs)])
    return num / den


def _diff_lambda(lam_ref, lam_init):
    lp = lam_ref[...]
    s1 = jnp.sum(lp[0:1] * lp[1:2], axis=-1, keepdims=True)
    s2 = jnp.sum(lp[2:3] * lp[3:4], axis=-1, keepdims=True)
    return jnp.exp(s1) - jnp.exp(s2) + lam_init


def _diff_head(q, ks, vs, lam, gsub, lam_init):
    lane = lax.broadcasted_iota(jnp.int32, q.shape, 1)
    zero = jnp.zeros_like(q)
    o0 = _attend(jnp.where(lane < HEAD_DIM, q, zero), ks, vs, [None] * len(ks))
    o1 = _attend(jnp.where(lane >= HEAD_DIM, q, zero), ks, vs, [None] * len(ks))
    o = o0 - lam * o1
    return (_rms(o, SUBLN_EPS) * gsub) * (1.0 - lam_init)


def _diff_kernel(lam_ref, gs_ref, q_ref, *refs, n_kv, lam_init):
    k_refs, v_refs, o_ref = refs[:n_kv], refs[n_kv:2 * n_kv], refs[2 * n_kv]
    lam = _diff_lambda(lam_ref, lam_init)
    o = _diff_head(q_ref[0], [r[0] for r in k_refs], [r[0] for r in v_refs], lam, gs_ref[...], lam_init)
    o_ref[0] = o.astype(BF)


def _diff_attention(lam_p, gsub, q_src, kv_srcs, *, tq, lam_init, name):
    b, sq, _ = q_src.shape
    hw = 2 * HEAD_DIM
    k0, v0 = A_QK // hw, 2 * A_QK // hw
    specs = [
        pl.BlockSpec(lam_p.shape, lambda i, h, t: (0, 0)),
        pl.BlockSpec((1, hw), lambda i, h, t: (0, 0)),
        pl.BlockSpec((1, tq, hw), lambda i, h, t: (i, t, h)),
    ]
    specs += [pl.BlockSpec((1, s.shape[1], hw), lambda i, h, t: (i, 0, k0 + h)) for s in kv_srcs]
    specs += [pl.BlockSpec((1, s.shape[1], hw), lambda i, h, t: (i, 0, v0 + h)) for s in kv_srcs]
    return pl.pallas_call(
        functools.partial(_diff_kernel, n_kv=len(kv_srcs), lam_init=lam_init),
        grid=(b, DIFF_HEADS, sq // tq),
        in_specs=specs,
        out_specs=pl.BlockSpec((1, tq, hw), lambda i, h, t: (i, t, h)),
        out_shape=jax.ShapeDtypeStruct((b, sq, A_V), BF),
        compiler_params=_cparams(("arbitrary", "arbitrary", "arbitrary")),
        name=name,
    )(lam_p, gsub, q_src, *kv_srcs, *kv_srcs)


def _na_kernel(q_ref, kl_ref, vl_ref, kc_ref, vc_ref, bias_ref, o_ref, *, rows):
    k = pl.program_id(1)
    start_row = jnp.clip(k * NA_QROWS - NA_KH // 2, 0, rows - NA_WIN_ROWS)
    nq = NA_QROWS * GRID_W
    win = NA_WIN_ROWS * GRID_W
    start = pl.multiple_of(start_row * GRID_W, nq)
    lane = lax.broadcasted_iota(jnp.int32, (nq, LANES), 1)
    for hp in range(NA_HEADS // 2):
        cs = slice(hp * LANES, (hp + 1) * LANES)
        q2 = q_ref[0, :, cs]
        ks = [kl_ref[0, pl.ds(start, win), cs], kc_ref[0, :, cs]]
        vs = [vl_ref[0, pl.ds(start, win), cs], vc_ref[0, :, cs]]
        zero = jnp.zeros_like(q2)
        o_lo = _attend(jnp.where(lane < HEAD_DIM, q2, zero), ks, vs, [bias_ref[0, 2 * hp], None])
        o_hi = _attend(jnp.where(lane >= HEAD_DIM, q2, zero), ks, vs, [bias_ref[0, 2 * hp + 1], None])
        o_ref[0, :, cs] = jnp.where(lane < HEAD_DIM, o_lo, o_hi).astype(BF)


def _na_bias_kind(k, n_steps):
    return jnp.where(k == 0, 0, jnp.where(k == n_steps - 1, 2, 1))


def _na_bias_table(rpb, rows):
    n_off = 2 * NA_KH - 1
    cq = np.arange(GRID_W)[:, None]
    ck = np.arange(GRID_W)[None, :]
    col_start = np.clip(cq - NA_KW // 2, 0, GRID_W - NA_KW)
    in_win = (ck >= col_start) & (ck < col_start + NA_KW)
    onehot = (ck - cq + (NA_KW - 1))[None] == np.arange(2 * NA_KW - 1)[:, None, None]
    tz = jnp.einsum('hrd,dqk->hrqk', rpb.astype(F32), jnp.asarray(onehot, F32), precision=lax.Precision.HIGHEST)
    tz = jnp.where(jnp.asarray(in_win)[None, None], tz, NEG_BIG)
    tz = jnp.concatenate([tz, jnp.full((NA_HEADS, 1, GRID_W, GRID_W), NEG_BIG, F32)], axis=1)
    n_steps = rows // NA_QROWS
    idx = np.full((3, NA_QROWS, NA_WIN_ROWS), n_off, np.int32)
    for kind, k in enumerate((0, 1, n_steps - 1)):
        start_row = min(max(k * NA_QROWS - NA_KH // 2, 0), rows - NA_WIN_ROWS)
        for q in range(NA_QROWS):
            r = k * NA_QROWS + q
            r_start = min(max(r - NA_KH // 2, 0), rows - NA_KH)
            for w in range(NA_WIN_ROWS):
                a = start_row + w
                if r_start <= a < r_start + NA_KH:
                    idx[kind, q, w] = a - r + (NA_KH - 1)
    t = jnp.take(tz, jnp.asarray(idx.reshape(-1)), axis=1)
    t = t.reshape(NA_HEADS, 3, NA_QROWS, NA_WIN_ROWS, GRID_W, GRID_W).transpose(1, 0, 2, 4, 3, 5)
    return t.reshape(3, NA_HEADS, NA_QROWS * GRID_W, NA_WIN_ROWS * GRID_W)


def _na_attention(p_lat, p_ctx, bias_tab, rows):
    b, s, _ = p_lat.shape
    lc = p_ctx.shape[1]
    base = (2 * A_QK + A_V) // B_QKV
    nq = NA_QROWS * GRID_W
    n_steps = rows // NA_QROWS
    return pl.pallas_call(
        functools.partial(_na_kernel, rows=rows),
        grid=(b, n_steps),
        in_specs=[
            pl.BlockSpec((1, nq, B_QKV), lambda i, k: (i, k, base)),
            pl.BlockSpec((1, s, B_QKV), lambda i, k: (i, 0, base + 1)),
            pl.BlockSpec((1, s, B_QKV), lambda i, k: (i, 0, base + 2)),
            pl.BlockSpec((1, lc, B_QKV), lambda i, k: (i, 0, base + 1)),
            pl.BlockSpec((1, lc, B_QKV), lambda i, k: (i, 0, base + 2)),
            pl.BlockSpec((1,) + bias_tab.shape[1:], lambda i, k: (_na_bias_kind(k, n_steps), 0, 0, 0)),
        ],
        out_specs=pl.BlockSpec((1, nq, B_QKV), lambda i, k: (i, k, 0)),
        out_shape=jax.ShapeDtypeStruct((b, s, B_QKV), BF),
        compiler_params=_cparams(("arbitrary", "arbitrary")),
        name="na_attn",
    )(p_lat, p_lat, p_lat, p_ctx, p_ctx, bias_tab)


def _dense_ctx_kernel(q_ref, k_ref, v_ref, o_ref):
    lane = lax.broadcasted_iota(jnp.int32, (q_ref.shape[1], LANES), 1)
    for hp in range(NA_HEADS // 2):
        cs = slice(hp * LANES, (hp + 1) * LANES)
        q2 = q_ref[0, :, cs]
        ks, vs = [k_ref[0, :, cs]], [v_ref[0, :, cs]]
        zero = jnp.zeros_like(q2)
        o_lo = _attend(jnp.where(lane < HEAD_DIM, q2, zero), ks, vs, [None])
        o_hi = _attend(jnp.where(lane >= HEAD_DIM, q2, zero), ks, vs, [None])
        o_ref[0, :, cs] = jnp.where(lane < HEAD_DIM, o_lo, o_hi).astype(BF)


def _dense_ctx_attention(p_ctx):
    b, lc, _ = p_ctx.shape
    base = (2 * A_QK + A_V) // B_QKV
    spec = lambda j: pl.BlockSpec((1, lc, B_QKV), lambda i: (i, 0, base + j))
    return pl.pallas_call(
        _dense_ctx_kernel,
        grid=(b,),
        in_specs=[spec(0), spec(1), spec(2)],
        out_specs=pl.BlockSpec((1, lc, B_QKV), lambda i: (i, 0, 0)),
        out_shape=jax.ShapeDtypeStruct((b, lc, B_QKV), BF),
        compiler_params=_cparams(("arbitrary",)),
        name="dense_ctx_attn",
    )(p_ctx, p_ctx, p_ctx)


def _mlp_residual(x1, n2g, sc2, sh2, g2, w1_ref, w2_ref):
    h2 = _modulated_norm(x1, n2g, sc2, sh2).astype(BF)
    hc = 1024
    acc = jnp.zeros_like(x1)
    for c in range(MLP_HIDDEN // hc):
        hid = jnp.maximum(_dot(h2, w1_ref[:, c * hc:(c + 1) * hc]), 0.0)
        acc = acc + _dot((hid * hid).astype(BF), w2_ref[c * hc:(c + 1) * hc, :])
    return x1 + g2 * acc


def _even_tail_kernel(oa_ref, ob_ref, x_ref, wo_ref, w1_ref, w2_ref, n2g_ref, g1_ref, sc2_ref, sh2_ref, g2_ref,
                      n1g_ref, sc1n_ref, sh1n_ref, xo_ref, hn_ref):
    y = _dot(oa_ref[0], wo_ref[0:A_V, :]) + _dot(ob_ref[0], wo_ref[A_V:, :])
    x1 = x_ref[0] + g1_ref[0] * y
    x2 = _mlp_residual(x1, n2g_ref[...], sc2_ref[0], sh2_ref[0], g2_ref[0], w1_ref, w2_ref)
    xo_ref[0] = x2
    hn_ref[0] = _modulated_norm(x2, n1g_ref[...], sc1n_ref[0], sh1n_ref[0])


def _even_tail(oa, ob, x, wo, w1, w2, n2g, g1, sc2, sh2, g2, n1g, sc1n, sh1n, *, tm):
    b, s, d = x.shape
    tok = lambda w: pl.BlockSpec((1, tm, w), lambda i, t: (i, t, 0))
    mod = pl.BlockSpec((1, 1, d), lambda i, t: (i, 0, 0))
    vec = pl.BlockSpec((1, d), lambda i, t: (0, 0))
    full = lambda a: pl.BlockSpec(a.shape, lambda i, t: (0, 0))
    return pl.pallas_call(
        _even_tail_kernel,
        grid=(b, s // tm),
        in_specs=[tok(A_V), tok(B_QKV), tok(d), full(wo), full(w1), full(w2), vec, mod, mod, mod, mod, vec, mod, mod],
        out_specs=[tok(d), tok(d)],
        out_shape=[jax.ShapeDtypeStruct((b, s, d), F32), jax.ShapeDtypeStruct((b, s, d), F32)],
        compiler_params=_cparams(("arbitrary", "arbitrary")),
        name="even_tail",
    )(oa, ob, x, wo, w1, w2, n2g, g1, sc2, sh2, g2, n1g, sc1n, sh1n)


def _odd_tail_kernel(y_ref, h_ref, x_ref, dsk_ref, wa_ref, wb_ref, w1_ref, w2_ref, n2g_ref, g1_ref, sc2_ref, sh2_ref,
                     g2_ref, fg_ref, o_ref):
    z = y_ref[0] + dsk_ref[...] * h_ref[0]
    z = jax.nn.gelu(z).astype(BF)
    mix = _dot(z, wa_ref[...]) * jax.nn.sigmoid(_dot(z, wb_ref[...]))
    x1 = x_ref[0] + g1_ref[0] * mix
    x2 = _mlp_residual(x1, n2g_ref[...], sc2_ref[0], sh2_ref[0], g2_ref[0], w1_ref, w2_ref)
    o_ref[0] = _rms(x2, NORM_EPS) * fg_ref[...]


def _odd_tail(y, h, x, dsk, wa, wb, w1, w2, n2g, g1, sc2, sh2, g2, fg, *, tm):
    b, s, d = x.shape
    tok = pl.BlockSpec((1, tm, d), lambda i, t: (i, t, 0))
    mod = pl.BlockSpec((1, 1, d), lambda i, t: (i, 0, 0))
    vec = pl.BlockSpec((1, d), lambda i, t: (0, 0))
    full = lambda a: pl.BlockSpec(a.shape, lambda i, t: (0, 0))
    return pl.pallas_call(
        _odd_tail_kernel,
        grid=(b, s // tm),
        in_specs=[tok, tok, tok, vec, full(wa), full(wb), full(w1), full(w2), vec, mod, mod, mod, mod, vec],
        out_specs=tok,
        out_shape=jax.ShapeDtypeStruct((b, s, d), F32),
        compiler_params=_cparams(("arbitrary", "arbitrary")),
        name="odd_tail",
    )(y, h, x, dsk, wa, wb, w1, w2, n2g, g1, sc2, sh2, g2, fg)


def _s5_weights(lam_re, lam_im, log_step, b_re, b_im, c_re, c_im):
    t_len = SSM_CHUNK
    hp = lax.Precision.HIGHEST
    lr = lam_re.astype(F32)
    li = lam_im.astype(F32)
    delta = jnp.exp(log_step.astype(F32))[..., None]
    k = jnp.arange(t_len + 1, dtype=F32)[None, None, :, None]
    mag = jnp.exp((lr * delta)[:, :, None, :] * k)
    ang = (li * delta)[:, :, None, :] * k
    ar_k = mag * jnp.cos(ang)
    ai_k = mag * jnp.sin(ang)
    ar1, ai1 = ar_k[:, :, 1], ai_k[:, :, 1]
    den = lr * lr + li * li
    nr = ar1 - 1.0
    cr = (nr * lr + ai1 * li) / den
    ci = (ai1 * lr - nr * li) / den
    br = b_re.astype(F32)
    bi = b_im.astype(F32)
    bbr = cr[..., None] * br - ci[..., None] * bi
    bbi = cr[..., None] * bi + ci[..., None] * br
    cre = c_re.astype(F32)[:, :, None]
    cim = c_im.astype(F32)[:, :, None]
    car = cre * ar_k[:, :, :, None, :] - cim * ai_k[:, :, :, None, :]
    cai = cre * ai_k[:, :, :, None, :] + cim * ar_k[:, :, :, None, :]
    kern = (jnp.einsum('dgkop,dgpi->dgkoi', car[:, :, :t_len], bbr, precision=hp)
            - jnp.einsum('dgkop,dgpi->dgkoi', cai[:, :, :t_len], bbi, precision=hp))
    abr = ar_k[:, :, :t_len, :, None] * bbr[:, :, None] - ai_k[:, :, :t_len, :, None] * bbi[:, :, None]
    abi = ar_k[:, :, :t_len, :, None] * bbi[:, :, None] + ai_k[:, :, :t_len, :, None] * bbr[:, :, None]

    g = lr.shape[1]
    s_idx = jnp.arange(t_len)[:, None]
    t_idx = jnp.arange(t_len)[None, :]

    def toeplitz(kd, lag, valid):
        m = kd[:, jnp.clip(lag, 0, t_len - 1)]
        m = jnp.where(valid[None, :, :, None, None], m, 0.0)
        return m.transpose(0, 1, 4, 2, 3).reshape(g, SSM_CW, SSM_CW)

    toep = toeplitz(kern[0], t_idx - s_idx, t_idx >= s_idx) + toeplitz(kern[1], s_idx - t_idx, s_idx >= t_idx)

    def in_op(ab, flip):
        ab = ab[:, ::-1] if flip else ab
        return ab.transpose(0, 1, 3, 2).reshape(g, SSM_CW, SSM_STATE)

    w_in = jnp.stack([in_op(abr[0], True), in_op(abi[0], True), in_op(abr[1], False), in_op(abi[1], False)], axis=2)

    def st_op(ca, lo, flip):
        ca = ca[:, lo:lo + t_len]
        ca = ca[:, ::-1] if flip else ca
        return ca.transpose(0, 3, 1, 2).reshape(g, SSM_STATE, SSM_CW)

    w_st = jnp.stack([st_op(car[0], 1, False), -st_op(cai[0], 1, False),
                      st_op(car[1], 1, True), -st_op(cai[1], 1, True)], axis=1)
    a_t = jnp.stack([ar_k[0, :, t_len], ai_k[0, :, t_len], ar_k[1, :, t_len], ai_k[1, :, t_len]], axis=1)

    eye = jnp.eye(2, dtype=F32)
    w_in_p = w_in.reshape(g // 2, 2, SSM_CW, 4, 1, SSM_STATE) * eye[None, :, None, None, :, None]
    w_in_p = w_in_p.reshape(g, SSM_CW, 8 * SSM_STATE)
    w_st_p = w_st.reshape(g // 2, 2, 4, 1, SSM_STATE, SSM_CW) * eye[None, :, None, :, None, None]
    w_st_p = w_st_p.reshape(g, 8 * SSM_STATE, SSM_CW)
    a_t_p = a_t.reshape(g // 2, 2, 4, SSM_STATE).transpose(0, 2, 1, 3).reshape(g // 2, 4, 2 * SSM_STATE)
    return toep.astype(BF), w_in_p.astype(BF), w_st_p.astype(BF), a_t_p


def _s5_kernel(ul_ref, uc_ref, toep_ref, win_ref, wst_ref, at_ref, y_ref, sloc_ref, sin_ref, *, n_ctx, n_lat, nb):
    w = 2 * SSM_STATE
    lat0 = n_ctx * nb
    sloc_ref[0:lat0, :] = _dot(uc_ref[0], win_ref[0]) + _dot(uc_ref[1], win_ref[1])
    sloc_ref[lat0:, :] = _dot(ul_ref[0], win_ref[0]) + _dot(ul_ref[1], win_ref[1])

    afr = jnp.broadcast_to(at_ref[0, 0:1, :], (nb, w))
    afi = jnp.broadcast_to(at_ref[0, 1:2, :], (nb, w))
    abr = jnp.broadcast_to(at_ref[0, 2:3, :], (nb, w))
    abi = jnp.broadcast_to(at_ref[0, 3:4, :], (nb, w))

    def advance(carry, rf, rb):
        fr, fi, br, bi = carry
        xfr = sloc_ref[pl.ds(rf, nb), 0:w]
        xfi = sloc_ref[pl.ds(rf, nb), w:2 * w]
        xbr = sloc_ref[pl.ds(rb, nb), 2 * w:3 * w]
        xbi = sloc_ref[pl.ds(rb, nb), 3 * w:4 * w]
        return (afr * fr - afi * fi + xfr, afr * fi + afi * fr + xfi,
                abr * br - abi * bi + xbr, abr * bi + abi * br + xbi)

    def ctx_body(i, carry):
        rf = pl.multiple_of(i * nb, nb)
        rb = pl.multiple_of((n_ctx - 1 - i) * nb, nb)
        return advance(carry, rf, rb)

    def lat_body(i, carry):
        fr, fi, br, bi = carry
        rf = pl.multiple_of(i * nb, nb)
        rb = pl.multiple_of((n_lat - 1 - i) * nb, nb)
        sin_ref[pl.ds(rf, nb), 0:w] = fr
        sin_ref[pl.ds(rf, nb), w:2 * w] = fi
        sin_ref[pl.ds(rb, nb), 2 * w:3 * w] = br
        sin_ref[pl.ds(rb, nb), 3 * w:4 * w] = bi
        return advance(carry, lat0 + rf, lat0 + rb)

    zero = jnp.zeros((nb, w), F32)
    carry = lax.fori_loop(0, n_ctx, ctx_body, (zero, zero, zero, zero))
    lax.fori_loop(0, n_lat, lat_body, carry)

    s_in = sin_ref[...].astype(BF)
    for j in range(2):
        y_ref[j] = _dot(ul_ref[j], toep_ref[j]) + _dot(s_in, wst_ref[j])


def _s5_scan(u_lat, u_ctx, toep, w_in, w_st, a_t, *, nb):
    g, rl, cw = u_lat.shape
    rc = u_ctx.shape[1]
    pair = lambda r, c: pl.BlockSpec((2, r, c), lambda i: (i, 0, 0))
    return pl.pallas_call(
        functools.partial(_s5_kernel, n_ctx=rc // nb, n_lat=rl // nb, nb=nb),
        grid=(g // 2,),
        in_specs=[pair(rl, cw), pair(rc, cw), pair(cw, cw), pair(cw, 8 * SSM_STATE), pair(8 * SSM_STATE, cw),
                  pl.BlockSpec((1, 4, 2 * SSM_STATE), lambda i: (i, 0, 0))],
        out_specs=pair(rl, cw),
        out_shape=jax.ShapeDtypeStruct((g, rl, cw), F32),
        scratch_shapes=[pltpu.VMEM((rl + rc, 8 * SSM_STATE), F32), pltpu.VMEM((rl, 8 * SSM_STATE), F32)],
        compiler_params=_cparams(("arbitrary",)),
        name="s5_scan",
    )(u_lat, u_ctx, toep, w_in, w_st, a_t)


def _to_group_major(h):
    b, n, _ = h.shape
    hb = h.astype(BF).reshape(b, n // SSM_CHUNK, SSM_CHUNK, SSM_GROUPS, SSM_GROUP)
    return hb.transpose(3, 1, 0, 2, 4).reshape(SSM_GROUPS, (n // SSM_CHUNK) * b, SSM_CW)


def _from_group_major(y, b):
    g, r, _ = y.shape
    n_chunk = r // b
    yy = y.reshape(g, n_chunk, b, SSM_CHUNK, SSM_GROUP).transpose(2, 1, 3, 0, 4)
    return yy.reshape(b, n_chunk * SSM_CHUNK, g * SSM_GROUP)


def kernel(x, c, ctx, c_ctx, w_ada, b_ada, norm1_g, norm2_g, final_g, w_in, w_out, lam_q1, lam_k1, lam_q2, lam_k2,
           subln_g, na_rpb, ssm_lam_re, ssm_lam_im, ssm_log_step, ssm_b_re, ssm_b_im, ssm_c_re, ssm_c_im, ssm_d,
           glu_w_a, glu_w_b, mlp_w1, mlp_w2):
    b, s, d = x.shape
    lc = ctx.shape[1]
    rows = s // GRID_W
    assert DEPTH == 2 and d == D_MODEL and s % GRID_W == 0 and rows >= NA_WIN_ROWS and rows % NA_QROWS == 0
    assert s % SSM_CHUNK == 0 and lc % SSM_CHUNK == 0 and b % SUBLANES == 0

    n_act = -(-(b + 1) // SUBLANES) * SUBLANES
    act_in = jnp.zeros((n_act, d), F32).at[:b].set(c).at[b].set(c_ctx)
    m_all = _ada(act_in, w_ada, b_ada)

    def mods(i):
        lat = [m_all[i, :b, j * d:(j + 1) * d].reshape(b, 1, d) for j in range(6)]
        cx = [jnp.broadcast_to(m_all[i, b, j * d:(j + 1) * d].reshape(1, 1, d), (b, 1, d)) for j in range(6)]
        return lat, cx

    (sh1, sc1, g1, sh2, sc2, g2), (sh1c, sc1c, g1c, sh2c, sc2c, g2c) = mods(0)
    (sh1n, sc1n, g1n, sh2n, sc2n, g2n), (sh1nc, sc1nc, _, _, _, _) = mods(1)
    n1g = [norm1_g[i].reshape(1, d) for i in range(DEPTH)]
    n2g = [norm2_g[i].reshape(1, d) for i in range(DEPTH)]

    cos_t, sin_t = _rope_tables(s)
    w_in_b = w_in[0].astype(BF)
    p_lat = _inproj(x, n1g[0], sc1, sh1, w_in_b, cos_t, sin_t, rope=True, tm=1024)
    p_ctx = _inproj(ctx, n1g[0], sc1c, sh1c, w_in_b, cos_t[:lc], sin_t[:lc], rope=False, tm=lc)

    lam_init = 0.8 - 0.6 * math.exp(-0.3 * 0)
    lam_p = jnp.stack([lam_q1[0], lam_k1[0], lam_q2[0], lam_k2[0]]).astype(F32)
    gsub = subln_g[0].reshape(1, 2 * HEAD_DIM).astype(F32)
    oa = _diff_attention(lam_p, gsub, p_lat, [p_lat, p_ctx], tq=512, lam_init=lam_init, name="diff_attn")
    oac = _diff_attention(lam_p, gsub, p_ctx, [p_ctx], tq=lc, lam_init=lam_init, name="diff_attn_ctx")
    ob = _na_attention(p_lat, p_ctx, _na_bias_table(na_rpb[0], rows), rows)
    obc = _dense_ctx_attention(p_ctx)

    wo_b = w_out[0].astype(BF)
    w1_b = [mlp_w1[i].astype(BF) for i in range(DEPTH)]
    w2_b = [mlp_w2[i].astype(BF) for i in range(DEPTH)]
    x1, h1 = _even_tail(oa, ob, x, wo_b, w1_b[0], w2_b[0], n2g[0], g1, sc2, sh2, g2, n1g[1], sc1n, sh1n, tm=512)
    _, hc1 = _even_tail(oac, obc, ctx, wo_b, w1_b[0], w2_b[0], n2g[0], g1c, sc2c, sh2c, g2c, n1g[1], sc1nc, sh1nc,
                        tm=lc)

    toep, w_si, w_st, a_t = _s5_weights(ssm_lam_re[0], ssm_lam_im[0], ssm_log_step[0], ssm_b_re[0], ssm_b_im[0],
                                        ssm_c_re[0], ssm_c_im[0])
    y_g = _s5_scan(_to_group_major(h1), _to_group_major(hc1), toep, w_si, w_st, a_t, nb=b)
    y = _from_group_major(y_g, b)
    return _odd_tail(y, h1, x1, ssm_d[0].reshape(1, d).astype(F32), glu_w_a[0].astype(BF), glu_w_b[0].astype(BF),
                     w1_b[1], w2_b[1], n2g[1], g1n, sc2n, sh2n, g2n, final_g.reshape(1, d).astype(F32), tm=512)
```

```python
import functools
import math

import jax
import jax.numpy as jnp
import numpy as np
from jax import lax
from jax.experimental import pallas as pl
from jax.experimental.pallas import tpu as pltpu

D_MODEL = 1024
DEPTH = 2
GRID_W = 64
HEAD_DIM = 64
DIFF_HEADS = 4
NA_HEADS = 8
NA_KH = 8
NA_KW = 16
NA_QROWS = 4
NA_WIN_ROWS = 12
ROPE_BASE = 10000.0
ROPE_AXIS_DIM = HEAD_DIM // 2
SSM_GROUP = 16
SSM_GROUPS = D_MODEL // SSM_GROUP
SSM_STATE = 64
MLP_HIDDEN = 4 * D_MODEL
A_QK = DIFF_HEADS * 2 * HEAD_DIM
A_V = DIFF_HEADS * 2 * HEAD_DIM
B_QKV = NA_HEADS * HEAD_DIM
IN_PROJ = 2 * A_QK + A_V + 3 * B_QKV
NORM_EPS = 1e-6
SUBLN_EPS = 1e-5

LANES = 128
SUBLANES = 8
SSM_CHUNK = 16
SSM_CW = SSM_CHUNK * SSM_GROUP
NEG_BIG = -1e30
VMEM_LIMIT = 56 * 1024 * 1024

BF = jnp.bfloat16
F32 = jnp.float32


def _cparams(sem):
    return pltpu.CompilerParams(dimension_semantics=sem, vmem_limit_bytes=VMEM_LIMIT)


def _dot(a, b):
    return jnp.dot(a, b, preferred_element_type=F32)


def _dot_nt(a, b):
    return lax.dot_general(a, b, (((1,), (1,)), ((), ())), preferred_element_type=F32)


def _rms(x, eps):
    return x * lax.rsqrt(jnp.mean(x * x, axis=-1, keepdims=True) + eps)


def _ada_kernel(a_ref, w_ref, b_ref, o_ref):
    a = a_ref[...]
    act = a * jax.nn.sigmoid(a)
    o_ref[0] = _dot(act.astype(BF), w_ref[0].astype(BF)) + b_ref[0]


def _ada(act_in, w_ada, b_ada):
    r, d = act_in.shape
    n = w_ada.shape[-1]
    tn = 1536
    return pl.pallas_call(
        _ada_kernel,
        grid=(DEPTH, n // tn),
        in_specs=[
            pl.BlockSpec((r, d), lambda i, j: (0, 0)),
            pl.BlockSpec((1, d, tn), lambda i, j: (i, 0, j)),
            pl.BlockSpec((1, 1, tn), lambda i, j: (i, 0, j)),
        ],
        out_specs=pl.BlockSpec((1, r, tn), lambda i, j: (i, 0, j)),
        out_shape=jax.ShapeDtypeStruct((DEPTH, r, n), F32),
        compiler_params=_cparams(("arbitrary", "arbitrary")),
        name="ada_ln",
    )(act_in, w_ada, b_ada.reshape(DEPTH, 1, n))


def _modulated_norm(x, g, sc, sh):
    return (_rms(x, NORM_EPS) * g) * (1.0 + sc) + sh


def _rope_chunk(x, cos, sin, low):
    up = pltpu.roll(x, LANES - 16, 1)
    dn = pltpu.roll(x, 16, 1)
    return x * cos + jnp.where(low, up, dn) * sin


def _inproj_kernel(x_ref, g_ref, sc_ref, sh_ref, w_ref, cos_ref, sin_ref, o_ref, *, rope):
    h = _modulated_norm(x_ref[0], g_ref[...], sc_ref[0], sh_ref[0]).astype(BF)
    tm = h.shape[0]
    if rope:
        cos = cos_ref[...]
        sin = sin_ref[...]
        low = (lax.broadcasted_iota(jnp.int32, (tm, LANES), 1) % 32) < 16
    cw = 512
    for j in range(IN_PROJ // cw):
        acc = _dot(h, w_ref[:, j * cw:(j + 1) * cw])
        if j in (0, 3):
            acc = acc * (HEAD_DIM ** -0.5)
        if rope and j < 2:
            parts = [_rope_chunk(acc[:, c * LANES:(c + 1) * LANES], cos, sin, low) for c in range(cw // LANES)]
            acc = jnp.concatenate(parts, axis=-1)
        o_ref[0, :, j * cw:(j + 1) * cw] = acc.astype(BF)


def _inproj(x, g, sc, sh, w, cos_t, sin_t, *, rope, tm):
    b, s, d = x.shape
    n = w.shape[-1]
    mod = pl.BlockSpec((1, 1, d), lambda i, t: (i, 0, 0))
    return pl.pallas_call(
        functools.partial(_inproj_kernel, rope=rope),
        grid=(b, s // tm),
        in_specs=[
            pl.BlockSpec((1, tm, d), lambda i, t: (i, t, 0)),
            pl.BlockSpec((1, d), lambda i, t: (0, 0)),
            mod, mod,
            pl.BlockSpec((d, n), lambda i, t: (0, 0)),
            pl.BlockSpec((tm, LANES), lambda i, t: (t, 0)),
            pl.BlockSpec((tm, LANES), lambda i, t: (t, 0)),
        ],
        out_specs=pl.BlockSpec((1, tm, n), lambda i, t: (i, t, 0)),
        out_shape=jax.ShapeDtypeStruct((b, s, n), BF),
        compiler_params=_cparams(("arbitrary", "arbitrary")),
        name="in_proj_rope" if rope else "in_proj_ctx",
    )(x, g, sc, sh, w, cos_t, sin_t)


def _rope_tables(seq):
    t = jnp.arange(seq)
    row = (t // GRID_W).astype(F32)
    col = (t % GRID_W).astype(F32)
    n_freq = ROPE_AXIS_DIM // 2
    inv_freq = ROPE_BASE ** (-jnp.arange(n_freq, dtype=F32) / n_freq)
    ang_r = row[:, None] * inv_freq[None, :]
    ang_c = col[:, None] * inv_freq[None, :]
    cos64 = jnp.concatenate([jnp.cos(ang_r), jnp.cos(ang_r), jnp.cos(ang_c), jnp.cos(ang_c)], axis=-1)
    sin64 = jnp.concatenate([-jnp.sin(ang_r), jnp.sin(ang_r), -jnp.sin(ang_c), jnp.sin(ang_c)], axis=-1)
    return jnp.tile(cos64, (1, 2)), jnp.tile(sin64, (1, 2))


def _attend(qm, ks, vs, biases):
    scores = [_dot_nt(qm, k) for k in ks]
    scores = [s if b is None else s + b for s, b in zip(scores, biases)]
    mx = functools.reduce(jnp.maximum, [jnp.max(s, axis=-1, keepdims=True) for s in scores])
    ps = [jnp.exp(s - mx) for s in scores]
    den = functools.reduce(jnp.add, [jnp.sum(p, axis=-1, keepdims=True) for p in ps])
    num = functools.reduce(jnp.add, [_dot(p.astype(BF), v) for p, v in zip(ps, vs)])
    return num / den


def _diff_lambda(lam_ref, lam_init):
    lp = lam_ref[...]
    s1 = jnp.sum(lp[0:1] * lp[1:2], axis=-1, keepdims=True)
    s2 = jnp.sum(lp[2:3] * lp[3:4], axis=-1, keepdims=True)
    return jnp.exp(s1) - jnp.exp(s2) + lam_init


def _diff_head(q, ks, vs, lam, gsub, lam_init):
    lane = lax.broadcasted_iota(jnp.int32, q.shape, 1)
    zero = jnp.zeros_like(q)
    o0 = _attend(jnp.where(lane < HEAD_DIM, q, zero), ks, vs, [None] * len(ks))
    o1 = _attend(jnp.where(lane >= HEAD_DIM, q, zero), ks, vs, [None] * len(ks))
    o = o0 - lam * o1
    return (_rms(o, SUBLN_EPS) * gsub) * (1.0 - lam_init)


def _diff_kernel(lam_ref, gs_ref, q_ref, *refs, n_kv, lam_init):
    k_refs, v_refs, o_ref = refs[:n_kv], refs[n_kv:2 * n_kv], refs[2 * n_kv]
    lam = _diff_lambda(lam_ref, lam_init)
    o = _diff_head(q_ref[0], [r[0] for r in k_refs], [r[0] for r in v_refs], lam, gs_ref[...], lam_init)
    o_ref[0] = o.astype(BF)


def _diff_attention(lam_p, gsub, q_src, kv_srcs, *, tq, lam_init, name):
    b, sq, _ = q_src.shape
    hw = 2 * HEAD_DIM
    k0, v0 = A_QK // hw, 2 * A_QK // hw
    specs = [
        pl.BlockSpec(lam_p.shape, lambda i, h, t: (0, 0)),
        pl.BlockSpec((1, hw), lambda i, h, t: (0, 0)),
        pl.BlockSpec((1, tq, hw), lambda i, h, t: (i, t, h)),
    ]
    specs += [pl.BlockSpec((1, s.shape[1], hw), lambda i, h, t: (i, 0, k0 + h)) for s in kv_srcs]
    specs += [pl.BlockSpec((1, s.shape[1], hw), lambda i, h, t: (i, 0, v0 + h)) for s in kv_srcs]
    return pl.pallas_call(
        functools.partial(_diff_kernel, n_kv=len(kv_srcs), lam_init=lam_init),
        grid=(b, DIFF_HEADS, sq // tq),
        in_specs=specs,
        out_specs=pl.BlockSpec((1, tq, hw), lambda i, h, t: (i, t, h)),
        out_shape=jax.ShapeDtypeStruct((b, sq, A_V), BF),
        compiler_params=_cparams(("arbitrary", "arbitrary", "arbitrary")),
        name=name,
    )(lam_p, gsub, q_src, *kv_srcs, *kv_srcs)


def _na_kernel(q_ref, kl_ref, vl_ref, kc_ref, vc_ref, bias_ref, o_ref, *, rows):
    k = pl.program_id(1)
    start_row = jnp.clip(k * NA_QROWS - NA_KH // 2, 0, rows - NA_WIN_ROWS)
    nq = NA_QROWS * GRID_W
    win = NA_WIN_ROWS * GRID_W
    start = pl.multiple_of(start_row * GRID_W, nq)
    lane = lax.broadcasted_iota(jnp.int32, (nq, LANES), 1)
    for hp in range(NA_HEADS // 2):
        cs = slice(hp * LANES, (hp + 1) * LANES)
        q2 = q_ref[0, :, cs]
        ks = [kl_ref[0, pl.ds(start, win), cs], kc_ref[0, :, cs]]
        vs = [vl_ref[0, pl.ds(start, win), cs], vc_ref[0, :, cs]]
        zero = jnp.zeros_like(q2)
        o_lo = _attend(jnp.where(lane < HEAD_DIM, q2, zero), ks, vs, [bias_ref[0, 2 * hp], None])
        o_hi = _attend(jnp.where(lane >= HEAD_DIM, q2, zero), ks, vs, [bias_ref[0, 2 * hp + 1], None])
        o_ref[0, :, cs] = jnp.where(lane < HEAD_DIM, o_lo, o_hi).astype(BF)


def _na_bias_kind(k, n_steps):
    return jnp.where(k == 0, 0, jnp.where(k == n_steps - 1, 2, 1))


def _na_bias_table(rpb, rows):
    n_off = 2 * NA_KH - 1
    cq = np.arange(GRID_W)[:, None]
    ck = np.arange(GRID_W)[None, :]
    col_start = np.clip(cq - NA_KW // 2, 0, GRID_W - NA_KW)
    in_win = (ck >= col_start) & (ck < col_start + NA_KW)
    onehot = (ck - cq + (NA_KW - 1))[None] == np.arange(2 * NA_KW - 1)[:, None, None]
    tz = jnp.einsum('hrd,dqk->hrqk', rpb.astype(F32), jnp.asarray(onehot, F32), precision=lax.Precision.HIGHEST)
    tz = jnp.where(jnp.asarray(in_win)[None, None], tz, NEG_BIG)
    tz = jnp.concatenate([tz, jnp.full((NA_HEADS, 1, GRID_W, GRID_W), NEG_BIG, F32)], axis=1)
    n_steps = rows // NA_QROWS
    idx = np.full((3, NA_QROWS, NA_WIN_ROWS), n_off, np.int32)
    for kind, k in enumerate((0, 1, n_steps - 1)):
        start_row = min(max(k * NA_QROWS - NA_KH // 2, 0), rows - NA_WIN_ROWS)
        for q in range(NA_QROWS):
            r = k * NA_QROWS + q
            r_start = min(max(r - NA_KH // 2, 0), rows - NA_KH)
            for w in range(NA_WIN_ROWS):
                a = start_row + w
                if r_start <= a < r_start + NA_KH:
                    idx[kind, q, w] = a - r + (NA_KH - 1)
    t = jnp.take(tz, jnp.asarray(idx.reshape(-1)), axis=1)
    t = t.reshape(NA_HEADS, 3, NA_QROWS, NA_WIN_ROWS, GRID_W, GRID_W).transpose(1, 0, 2, 4, 3, 5)
    return t.reshape(3, NA_HEADS, NA_QROWS * GRID_W, NA_WIN_ROWS * GRID_W)


def _na_attention(p_lat, p_ctx, bias_tab, rows):
    b, s, _ = p_lat.shape
    lc = p_ctx.shape[1]
    base = (2 * A_QK + A_V) // B_QKV
    nq = NA_QROWS * GRID_W
    n_steps = rows // NA_QROWS
    return pl.pallas_call(
        functools.partial(_na_kernel, rows=rows),
        grid=(b, n_steps),
        in_specs=[
            pl.BlockSpec((1, nq, B_QKV), lambda i, k: (i, k, base)),
            pl.BlockSpec((1, s, B_QKV), lambda i, k: (i, 0, base + 1)),
            pl.BlockSpec((1, s, B_QKV), lambda i, k: (i, 0, base + 2)),
            pl.BlockSpec((1, lc, B_QKV), lambda i, k: (i, 0, base + 1)),
            pl.BlockSpec((1, lc, B_QKV), lambda i, k: (i, 0, base + 2)),
            pl.BlockSpec((1,) + bias_tab.shape[1:], lambda i, k: (_na_bias_kind(k, n_steps), 0, 0, 0)),
        ],
        out_specs=pl.BlockSpec((1, nq, B_QKV), lambda i, k: (i, k, 0)),
        out_shape=jax.ShapeDtypeStruct((b, s, B_QKV), BF),
        compiler_params=_cparams(("arbitrary", "arbitrary")),
        name="na_attn",
    )(p_lat, p_lat, p_lat, p_ctx, p_ctx, bias_tab)


def _dense_ctx_kernel(q_ref, k_ref, v_ref, o_ref):
    lane = lax.broadcasted_iota(jnp.int32, (q_ref.shape[1], LANES), 1)
    for hp in range(NA_HEADS // 2):
        cs = slice(hp * LANES, (hp + 1) * LANES)
        q2 = q_ref[0, :, cs]
        ks, vs = [k_ref[0, :, cs]], [v_ref[0, :, cs]]
        zero = jnp.zeros_like(q2)
        o_lo = _attend(jnp.where(lane < HEAD_DIM, q2, zero), ks, vs, [None])
        o_hi = _attend(jnp.where(lane >= HEAD_DIM, q2, zero), ks, vs, [None])
        o_ref[0, :, cs] = jnp.where(lane < HEAD_DIM, o_lo, o_hi).astype(BF)


def _dense_ctx_attention(p_ctx):
    b, lc, _ = p_ctx.shape
    base = (2 * A_QK + A_V) // B_QKV
    spec = lambda j: pl.BlockSpec((1, lc, B_QKV), lambda i: (i, 0, base + j))
    return pl.pallas_call(
        _dense_ctx_kernel,
        grid=(b,),
        in_specs=[spec(0), spec(1), spec(2)],
        out_specs=pl.BlockSpec((1, lc, B_QKV), lambda i: (i, 0, 0)),
        out_shape=jax.ShapeDtypeStruct((b, lc, B_QKV), BF),
        compiler_params=_cparams(("arbitrary",)),
        name="dense_ctx_attn",
    )(p_ctx, p_ctx, p_ctx)


def _mlp_residual(x1, n2g, sc2, sh2, g2, w1_ref, w2_ref):
    h2 = _modulated_norm(x1, n2g, sc2, sh2).astype(BF)
    hc = 1024
    acc = jnp.zeros_like(x1)
    for c in range(MLP_HIDDEN // hc):
        hid = jnp.maximum(_dot(h2, w1_ref[:, c * hc:(c + 1) * hc]), 0.0)
        acc = acc + _dot((hid * hid).astype(BF), w2_ref[c * hc:(c + 1) * hc, :])
    return x1 + g2 * acc


def _even_tail_kernel(oa_ref, ob_ref, x_ref, wo_ref, w1_ref, w2_ref, n2g_ref, g1_ref, sc2_ref, sh2_ref, g2_ref,
                      n1g_ref, sc1n_ref, sh1n_ref, xo_ref, hn_ref):
    y = _dot(oa_ref[0], wo_ref[0:A_V, :]) + _dot(ob_ref[0], wo_ref[A_V:, :])
    x1 = x_ref[0] + g1_ref[0] * y
    x2 = _mlp_residual(x1, n2g_ref[...], sc2_ref[0], sh2_ref[0], g2_ref[0], w1_ref, w2_ref)
    xo_ref[0] = x2
    hn_ref[0] = _modulated_norm(x2, n1g_ref[...], sc1n_ref[0], sh1n_ref[0])


def _even_tail(oa, ob, x, wo, w1, w2, n2g, g1, sc2, sh2, g2, n1g, sc1n, sh1n, *, tm):
    b, s, d = x.shape
    tok = lambda w: pl.BlockSpec((1, tm, w), lambda i, t: (i, t, 0))
    mod = pl.BlockSpec((1, 1, d), lambda i, t: (i, 0, 0))
    vec = pl.BlockSpec((1, d), lambda i, t: (0, 0))
    full = lambda a: pl.BlockSpec(a.shape, lambda i, t: (0, 0))
    return pl.pallas_call(
        _even_tail_kernel,
        grid=(b, s // tm),
        in_specs=[tok(A_V), tok(B_QKV), tok(d), full(wo), full(w1), full(w2), vec, mod, mod, mod, mod, vec, mod, mod],
        out_specs=[tok(d), tok(d)],
        out_shape=[jax.ShapeDtypeStruct((b, s, d), F32), jax.ShapeDtypeStruct((b, s, d), F32)],
        compiler_params=_cparams(("arbitrary", "arbitrary")),
        name="even_tail",
    )(oa, ob, x, wo, w1, w2, n2g, g1, sc2, sh2, g2, n1g, sc1n, sh1n)


def _odd_tail_kernel(y_ref, h_ref, x_ref, dsk_ref, wa_ref, wb_ref, w1_ref, w2_ref, n2g_ref, g1_ref, sc2_ref, sh2_ref,
                     g2_ref, fg_ref, o_ref):
    z = y_ref[0] + dsk_ref[...] * h_ref[0]
    z = jax.nn.gelu(z).astype(BF)
    mix = _dot(z, wa_ref[...]) * jax.nn.sigmoid(_dot(z, wb_ref[...]))
    x1 = x_ref[0] + g1_ref[0] * mix
    x2 = _mlp_residual(x1, n2g_ref[...], sc2_ref[0], sh2_ref[0], g2_ref[0], w1_ref, w2_ref)
    o_ref[0] = _rms(x2, NORM_EPS) * fg_ref[...]


def _odd_tail(y, h, x, dsk, wa, wb, w1, w2, n2g, g1, sc2, sh2, g2, fg, *, tm):
    b, s, d = x.shape
    tok = pl.BlockSpec((1, tm, d), lambda i, t: (i, t, 0))
    mod = pl.BlockSpec((1, 1, d), lambda i, t: (i, 0, 0))
    vec = pl.BlockSpec((1, d), lambda i, t: (0, 0))
    full = lambda a: pl.BlockSpec(a.shape, lambda i, t: (0, 0))
    return pl.pallas_call(
        _odd_tail_kernel,
        grid=(b, s // tm),
        in_specs=[tok, tok, tok, vec, full(wa), full(wb), full(w1), full(w2), vec, mod, mod, mod, mod, vec],
        out_specs=tok,
        out_shape=jax.ShapeDtypeStruct((b, s, d), F32),
        compiler_params=_cparams(("arbitrary", "arbitrary")),
        name="odd_tail",
    )(y, h, x, dsk, wa, wb, w1, w2, n2g, g1, sc2, sh2, g2, fg)


def _slot_tables():
    q = np.arange(SSM_GROUPS)
    m = q % SUBLANES
    group = (q // SUBLANES) * SUBLANES + (SUBLANES - m) % SUBLANES
    pos = np.arange(SSM_CHUNK)[None, :]
    octet = (pos // SUBLANES) * SUBLANES
    tok_in = octet + (pos % SUBLANES + m[:, None]) % SUBLANES
    tok_out = octet + ((SUBLANES - m[:, None]) % SUBLANES - pos % SUBLANES) % SUBLANES
    return group, tok_in, tok_out


def _s5_operators(lam_re, lam_im, log_step, b_re, b_im, c_re, c_im):
    t_len = SSM_CHUNK
    hp = lax.Precision.HIGHEST
    group, tok_in, tok_out = _slot_tables()
    g = SSM_GROUPS
    lr = lam_re.astype(F32)[:, group]
    li = lam_im.astype(F32)[:, group]
    delta = jnp.exp(log_step.astype(F32))[:, group, None]
    dec = lr * delta
    ang = li * delta
    mag1 = jnp.exp(dec)
    ar1 = mag1 * jnp.cos(ang)
    ai1 = mag1 * jnp.sin(ang)
    den = lr * lr + li * li
    nr = ar1 - 1.0
    cr = (nr * lr + ai1 * li) / den
    ci = (ai1 * lr - nr * li) / den
    br = b_re.astype(F32)[:, group]
    bi = b_im.astype(F32)[:, group]
    bbr = (cr[..., None] * br - ci[..., None] * bi).transpose(0, 1, 3, 2)
    bbi = (cr[..., None] * bi + ci[..., None] * br).transpose(0, 1, 3, 2)
    cre = c_re.astype(F32)[:, group]
    cim = c_im.astype(F32)[:, group]

    def apow(d, e):
        e = jnp.asarray(e, F32)[:, :, None]
        mag = jnp.exp(dec[d][:, None, :] * e)
        return mag * jnp.cos(ang[d][:, None, :] * e), mag * jnp.sin(ang[d][:, None, :] * e)

    def in_factor(d, e):
        ar, ai = apow(d, e)
        re = ar[:, :, None, :] * bbr[d][:, None] - ai[:, :, None, :] * bbi[d][:, None]
        im = ar[:, :, None, :] * bbi[d][:, None] + ai[:, :, None, :] * bbr[d][:, None]
        return re.reshape(g, SSM_CW, SSM_STATE), im.reshape(g, SSM_CW, SSM_STATE)

    def out_factor(d, e):
        ar, ai = apow(d, e)
        ar = ar.transpose(0, 2, 1)[:, :, :, None]
        ai = ai.transpose(0, 2, 1)[:, :, :, None]
        ct_r = cre[d].transpose(0, 2, 1)[:, :, None, :]
        ct_i = cim[d].transpose(0, 2, 1)[:, :, None, :]
        re = ar * ct_r - ai * ct_i
        im = ar * ct_i + ai * ct_r
        return re.reshape(g, SSM_STATE, SSM_CW), (-im).reshape(g, SSM_STATE, SSM_CW)

    def toeplitz(d, e_in, e_out, valid):
        lre, lim = in_factor(d, e_in)
        rre, rim = out_factor(d, e_out)
        full = jnp.einsum('gxp,gpy->gxy', jnp.concatenate([lre, lim], axis=-1), jnp.concatenate([rre, rim], axis=1),
                          precision=hp)
        mask = np.broadcast_to(valid[:, :, None, :, None], (g, t_len, SSM_GROUP, t_len, SSM_GROUP))
        return jnp.where(jnp.asarray(mask.reshape(g, SSM_CW, SSM_CW)), full, 0.0)

    s_in = tok_in[:, :, None]
    t_out = tok_out[:, None, :]
    toep = toeplitz(0, -tok_in, tok_out, t_out >= s_in) + toeplitz(1, tok_in, -tok_out, s_in >= t_out)

    w_in = jnp.stack(in_factor(0, t_len - 1 - tok_in) + in_factor(1, tok_in), axis=2)
    w_st = jnp.stack(out_factor(0, tok_out + 1) + out_factor(1, t_len - tok_out), axis=1)
    full_t = np.full((g, 1), t_len)
    a_t = jnp.concatenate(apow(0, full_t) + apow(1, full_t), axis=1)

    eye = jnp.eye(2, dtype=F32)
    w_in_p = w_in.reshape(g // 2, 2, SSM_CW, 4, 1, SSM_STATE) * eye[None, :, None, None, :, None]
    w_in_p = w_in_p.reshape(g, SSM_CW, 8 * SSM_STATE)
    w_st_p = w_st.reshape(g // 2, 2, 4, 1, SSM_STATE, SSM_CW) * eye[None, :, None, :, None, None]
    w_st_p = w_st_p.reshape(g, 8 * SSM_STATE, SSM_CW)
    a_t_p = a_t.reshape(g // 2, 2, 4, SSM_STATE).transpose(0, 2, 1, 3).reshape(g // 2, 4, 2 * SSM_STATE)
    return toep.astype(BF), w_in_p.astype(BF), w_st_p.astype(BF), a_t_p


def _to_slots(x):
    a = pltpu.roll(x, 0, 1, stride=SSM_GROUP, stride_axis=0)
    a = a.reshape(x.shape[0] // SUBLANES, SUBLANES, LANES)
    blk = lax.broadcasted_iota(jnp.int32, a.shape, 2) // SSM_GROUP
    for bit in (1, 2, 4):
        a = jnp.where((blk & bit) != 0, pltpu.roll(a, SUBLANES - bit, 1), a)
    return a


def _from_slots(t):
    blk = lax.broadcasted_iota(jnp.int32, t.shape, 2) // SSM_GROUP
    for bit in (1, 2, 4):
        t = jnp.where((blk & bit) != 0, pltpu.roll(t, SUBLANES - bit, 1), t)
    a = t.reshape(t.shape[0] * SUBLANES, LANES)
    return pltpu.roll(a, 0, 1, stride=SSM_GROUP, stride_axis=0)


def _s5_pack_kernel(h_ref, hc_ref, o_ref, *, n_ctx, n_lat):
    def body(b, carry):
        tc = _to_slots(hc_ref[b]).reshape(n_ctx, 2, SUBLANES, LANES)
        tl = _to_slots(h_ref[b]).reshape(n_lat, 2, SUBLANES, LANES)
        for o in range(2):
            o_ref[0, o, 0:n_ctx, b] = tc[:, o]
            o_ref[0, o, n_ctx:, b] = tl[:, o]
        return carry

    lax.fori_loop(0, h_ref.shape[0], body, 0)


def _s5_pack(h, hc):
    b, s, d = h.shape
    lc = hc.shape[1]
    n_ctx, n_lat = lc // SSM_CHUNK, s // SSM_CHUNK
    shape = (d // LANES, 2, n_ctx + n_lat, b, SUBLANES, LANES)
    return pl.pallas_call(
        functools.partial(_s5_pack_kernel, n_ctx=n_ctx, n_lat=n_lat),
        grid=(d // LANES,),
        in_specs=[pl.BlockSpec((b, s, LANES), lambda j: (0, 0, j)), pl.BlockSpec((b, lc, LANES), lambda j: (0, 0, j))],
        out_specs=pl.BlockSpec((1,) + shape[1:], lambda j: (j, 0, 0, 0, 0, 0)),
        out_shape=jax.ShapeDtypeStruct(shape, F32),
        compiler_params=_cparams(("arbitrary",)),
        name="s5_pack",
    )(h, hc)


def _s5_unpack_kernel(y_ref, o_ref, *, n_lat):
    def body(b, carry):
        t = jnp.stack([y_ref[0, 0, :, b], y_ref[0, 1, :, b]], axis=1).reshape(n_lat * 2, SUBLANES, LANES)
        o_ref[b] = _from_slots(t)
        return carry

    lax.fori_loop(0, o_ref.shape[0], body, 0)


def _s5_unpack(y, b):
    nj, _, n_lat = y.shape[:3]
    s = n_lat * SSM_CHUNK
    return pl.pallas_call(
        functools.partial(_s5_unpack_kernel, n_lat=n_lat),
        grid=(nj,),
        in_specs=[pl.BlockSpec((1,) + y.shape[1:], lambda j: (j, 0, 0, 0, 0, 0))],
        out_specs=pl.BlockSpec((b, s, LANES), lambda j: (0, 0, j)),
        out_shape=jax.ShapeDtypeStruct((b, s, nj * LANES), F32),
        compiler_params=_cparams(("arbitrary",)),
        name="s5_unpack",
    )(y)


def _s5_kernel(u_ref, toep_ref, win_ref, wst_ref, at_ref, y_ref, sloc_ref, sin_ref, *, n_ctx, n_lat, nb):
    p = pl.program_id(1)
    w = 2 * SSM_STATE
    n_rows = (n_ctx + n_lat) * nb
    lat0 = n_ctx * nb
    slots = [2 * p, 2 * p + 1]
    us = [jnp.concatenate([u_ref[0, o, pl.ds(m, n_rows, stride=SUBLANES), :] for o in range(2)], axis=-1).astype(BF)
          for m in slots]
    sloc_ref[...] = _dot(us[0], win_ref[0]) + _dot(us[1], win_ref[1])

    afr = jnp.broadcast_to(at_ref[0, 0:1, :], (nb, w))
    afi = jnp.broadcast_to(at_ref[0, 1:2, :], (nb, w))
    abr = jnp.broadcast_to(at_ref[0, 2:3, :], (nb, w))
    abi = jnp.broadcast_to(at_ref[0, 3:4, :], (nb, w))

    def advance(carry, rf, rb):
        fr, fi, br, bi = carry
        xfr = sloc_ref[pl.ds(rf, nb), 0:w]
        xfi = sloc_ref[pl.ds(rf, nb), w:2 * w]
        xbr = sloc_ref[pl.ds(rb, nb), 2 * w:3 * w]
        xbi = sloc_ref[pl.ds(rb, nb), 3 * w:4 * w]
        return (afr * fr - afi * fi + xfr, afr * fi + afi * fr + xfi,
                abr * br - abi * bi + xbr, abr * bi + abi * br + xbi)

    def ctx_body(i, carry):
        rf = pl.multiple_of(i * nb, nb)
        rb = pl.multiple_of((n_ctx - 1 - i) * nb, nb)
        return advance(carry, rf, rb)

    def lat_body(i, carry):
        fr, fi, br, bi = carry
        rf = pl.multiple_of(i * nb, nb)
        rb = pl.multiple_of((n_lat - 1 - i) * nb, nb)
        sin_ref[pl.ds(rf, nb), 0:w] = fr
        sin_ref[pl.ds(rf, nb), w:2 * w] = fi
        sin_ref[pl.ds(rb, nb), 2 * w:3 * w] = br
        sin_ref[pl.ds(rb, nb), 3 * w:4 * w] = bi
        return advance(carry, lat0 + rf, lat0 + rb)

    zero = jnp.zeros((nb, w), F32)
    carry = lax.fori_loop(0, n_ctx, ctx_body, (zero, zero, zero, zero))
    lax.fori_loop(0, n_lat, lat_body, carry)

    s_in = sin_ref[...].astype(BF)
    for q in range(2):
        y = _dot(us[q][lat0:], toep_ref[q]) + _dot(s_in, wst_ref[q])
        m_out = (SUBLANES - slots[q]) % SUBLANES
        for o in range(2):
            y_ref[0, o, pl.ds(m_out, n_lat * nb, stride=SUBLANES), :] = y[:, o * LANES:(o + 1) * LANES]


def _s5_scan(u, toep, w_in, w_st, a_t, *, n_ctx, nb):
    nj, _, r8, _ = u.shape
    n_lat = r8 // (nb * SUBLANES) - n_ctx
    cw = SSM_CW
    n_pairs = SUBLANES // 2
    pair = lambda r, c: pl.BlockSpec((2, r, c), lambda j, p: (j * n_pairs + p, 0, 0))
    return pl.pallas_call(
        functools.partial(_s5_kernel, n_ctx=n_ctx, n_lat=n_lat, nb=nb),
        grid=(nj, n_pairs),
        in_specs=[pl.BlockSpec((1, 2, r8, LANES), lambda j, p: (j, 0, 0, 0)),
                  pair(cw, cw), pair(cw, 8 * SSM_STATE), pair(8 * SSM_STATE, cw),
                  pl.BlockSpec((1, 4, 2 * SSM_STATE), lambda j, p: (j * n_pairs + p, 0, 0))],
        out_specs=pl.BlockSpec((1, 2, n_lat * nb * SUBLANES, LANES), lambda j, p: (j, 0, 0, 0)),
        out_shape=jax.ShapeDtypeStruct((nj, 2, n_lat * nb * SUBLANES, LANES), F32),
        scratch_shapes=[pltpu.VMEM(((n_ctx + n_lat) * nb, 8 * SSM_STATE), F32),
                        pltpu.VMEM((n_lat * nb, 8 * SSM_STATE), F32)],
        compiler_params=_cparams(("arbitrary", "arbitrary")),
        name="s5_scan",
    )(u, toep, w_in, w_st, a_t)


def _s5_mix(h, hc, params):
    b, s, d = h.shape
    n_ctx = hc.shape[1] // SSM_CHUNK
    n_lat = s // SSM_CHUNK
    toep, w_si, w_st, a_t = _s5_operators(*params)
    u = _s5_pack(h, hc).reshape(d // LANES, 2, (n_ctx + n_lat) * b * SUBLANES, LANES)
    y = _s5_scan(u, toep, w_si, w_st, a_t, n_ctx=n_ctx, nb=b)
    return _s5_unpack(y.reshape(d // LANES, 2, n_lat, b, SUBLANES, LANES), b)


def kernel(x, c, ctx, c_ctx, w_ada, b_ada, norm1_g, norm2_g, final_g, w_in, w_out, lam_q1, lam_k1, lam_q2, lam_k2,
           subln_g, na_rpb, ssm_lam_re, ssm_lam_im, ssm_log_step, ssm_b_re, ssm_b_im, ssm_c_re, ssm_c_im, ssm_d,
           glu_w_a, glu_w_b, mlp_w1, mlp_w2):
    b, s, d = x.shape
    lc = ctx.shape[1]
    rows = s // GRID_W
    assert DEPTH == 2 and d == D_MODEL and s % GRID_W == 0 and rows >= NA_WIN_ROWS and rows % NA_QROWS == 0
    assert s % SSM_CHUNK == 0 and lc % SSM_CHUNK == 0 and b % SUBLANES == 0

    n_act = -(-(b + 1) // SUBLANES) * SUBLANES
    act_in = jnp.zeros((n_act, d), F32).at[:b].set(c).at[b].set(c_ctx)
    m_all = _ada(act_in, w_ada, b_ada)

    def mods(i):
        lat = [m_all[i, :b, j * d:(j + 1) * d].reshape(b, 1, d) for j in range(6)]
        cx = [jnp.broadcast_to(m_all[i, b, j * d:(j + 1) * d].reshape(1, 1, d), (b, 1, d)) for j in range(6)]
        return lat, cx

    (sh1, sc1, g1, sh2, sc2, g2), (sh1c, sc1c, g1c, sh2c, sc2c, g2c) = mods(0)
    (sh1n, sc1n, g1n, sh2n, sc2n, g2n), (sh1nc, sc1nc, _, _, _, _) = mods(1)
    n1g = [norm1_g[i].reshape(1, d) for i in range(DEPTH)]
    n2g = [norm2_g[i].reshape(1, d) for i in range(DEPTH)]

    cos_t, sin_t = _rope_tables(s)
    w_in_b = w_in[0].astype(BF)
    p_lat = _inproj(x, n1g[0], sc1, sh1, w_in_b, cos_t, sin_t, rope=True, tm=1024)
    p_ctx = _inproj(ctx, n1g[0], sc1c, sh1c, w_in_b, cos_t[:lc], sin_t[:lc], rope=False, tm=lc)

    lam_init = 0.8 - 0.6 * math.exp(-0.3 * 0)
    lam_p = jnp.stack([lam_q1[0], lam_k1[0], lam_q2[0], lam_k2[0]]).astype(F32)
    gsub = subln_g[0].reshape(1, 2 * HEAD_DIM).astype(F32)
    oa = _diff_attention(lam_p, gsub, p_lat, [p_lat, p_ctx], tq=512, lam_init=lam_init, name="diff_attn")
    oac = _diff_attention(lam_p, gsub, p_ctx, [p_ctx], tq=lc, lam_init=lam_init, name="diff_attn_ctx")
    ob = _na_attention(p_lat, p_ctx, _na_bias_table(na_rpb[0], rows), rows)
    obc = _dense_ctx_attention(p_ctx)

    wo_b = w_out[0].astype(BF)
    w1_b = [mlp_w1[i].astype(BF) for i in range(DEPTH)]
    w2_b = [mlp_w2[i].astype(BF) for i in range(DEPTH)]
    x1, h1 = _even_tail(oa, ob, x, wo_b, w1_b[0], w2_b[0], n2g[0], g1, sc2, sh2, g2, n1g[1], sc1n, sh1n, tm=512)
    _, hc1 = _even_tail(oac, obc, ctx, wo_b, w1_b[0], w2_b[0], n2g[0], g1c, sc2c, sh2c, g2c, n1g[1], sc1nc, sh1nc,
                        tm=lc)

    y = _s5_mix(h1, hc1, (ssm_lam_re[0], ssm_lam_im[0], ssm_log_step[0], ssm_b_re[0], ssm_b_im[0], ssm_c_re[0],
                          ssm_c_im[0]))
    return _odd_tail(y, h1, x1, ssm_d[0].reshape(1, d).astype(F32), glu_w_a[0].astype(BF), glu_w_b[0].astype(BF),
                     w1_b[1], w2_b[1], n2g[1], g1n, sc2n, sh2n, g2n, final_g.reshape(1, d).astype(F32), tm=512)
```

```python
import functools
import math

import jax
import jax.numpy as jnp
import numpy as np
from jax import lax
from jax.experimental import pallas as pl
from jax.experimental.pallas import tpu as pltpu

D_MODEL = 1024
DEPTH = 2
GRID_W = 64
HEAD_DIM = 64
DIFF_HEADS = 4
NA_HEADS = 8
NA_KH = 8
NA_KW = 16
NA_QROWS = 4
NA_WIN_ROWS = 12
ATTN_Q_SUB = 256
ROPE_BASE = 10000.0
ROPE_AXIS_DIM = HEAD_DIM // 2
SSM_GROUP = 16
SSM_GROUPS = D_MODEL // SSM_GROUP
SSM_STATE = 64
MLP_HIDDEN = 4 * D_MODEL
A_QK = DIFF_HEADS * 2 * HEAD_DIM
A_V = DIFF_HEADS * 2 * HEAD_DIM
B_QKV = NA_HEADS * HEAD_DIM
IN_PROJ = 2 * A_QK + A_V + 3 * B_QKV
NORM_EPS = 1e-6
SUBLN_EPS = 1e-5

LANES = 128
SUBLANES = 8
SSM_CHUNK = 16
SSM_CW = SSM_CHUNK * SSM_GROUP
NEG_BIG = -1e30
LOG2E = math.log2(math.e)
VMEM_LIMIT = 56 * 1024 * 1024

BF = jnp.bfloat16
F32 = jnp.float32


def _cparams(sem):
    return pltpu.CompilerParams(dimension_semantics=sem, vmem_limit_bytes=VMEM_LIMIT)


def _dot(a, b):
    return jnp.dot(a, b, preferred_element_type=F32)


def _dot_nt(a, b):
    return lax.dot_general(a, b, (((1,), (1,)), ((), ())), preferred_element_type=F32)


def _dot_tn(a, b):
    return lax.dot_general(a, b, (((0,), (0,)), ((), ())), preferred_element_type=F32)


def _rms(x, eps):
    return x * lax.rsqrt(jnp.mean(x * x, axis=-1, keepdims=True) + eps)


def _ada_kernel(a_ref, w_ref, b_ref, o_ref):
    a = a_ref[...]
    act = a * jax.nn.sigmoid(a)
    o_ref[0] = _dot(act.astype(BF), w_ref[0].astype(BF)) + b_ref[0]


def _ada(act_in, w_ada, b_ada):
    r, d = act_in.shape
    n = w_ada.shape[-1]
    tn = 1536
    return pl.pallas_call(
        _ada_kernel,
        grid=(DEPTH, n // tn),
        in_specs=[
            pl.BlockSpec((r, d), lambda i, j: (0, 0)),
            pl.BlockSpec((1, d, tn), lambda i, j: (i, 0, j)),
            pl.BlockSpec((1, 1, tn), lambda i, j: (i, 0, j)),
        ],
        out_specs=pl.BlockSpec((1, r, tn), lambda i, j: (i, 0, j)),
        out_shape=jax.ShapeDtypeStruct((DEPTH, r, n), F32),
        compiler_params=_cparams(("arbitrary", "arbitrary")),
        name="ada_ln",
    )(act_in, w_ada, b_ada.reshape(DEPTH, 1, n))


def _modulated_norm(x, g, sc, sh):
    return (_rms(x, NORM_EPS) * g) * (1.0 + sc) + sh


def _rope_chunk(x, cos, sin, low):
    up = pltpu.roll(x, LANES - 16, 1)
    dn = pltpu.roll(x, 16, 1)
    return x * cos + jnp.where(low, up, dn) * sin


def _inproj_kernel(x_ref, g_ref, sc_ref, sh_ref, w_ref, cos_ref, sin_ref, o_ref, *, rope):
    h = _modulated_norm(x_ref[0], g_ref[...], sc_ref[0], sh_ref[0]).astype(BF)
    tm = h.shape[0]
    if rope:
        cos = cos_ref[...]
        sin = sin_ref[...]
        low = (lax.broadcasted_iota(jnp.int32, (tm, LANES), 1) % 32) < 16
    cw = 512
    for j in range(IN_PROJ // cw):
        acc = _dot(h, w_ref[:, j * cw:(j + 1) * cw])
        if j in (0, 3):
            acc = acc * (HEAD_DIM ** -0.5 * LOG2E)
        if rope and j < 2:
            parts = [_rope_chunk(acc[:, c * LANES:(c + 1) * LANES], cos, sin, low) for c in range(cw // LANES)]
            acc = jnp.concatenate(parts, axis=-1)
        o_ref[0, :, j * cw:(j + 1) * cw] = acc.astype(BF)


def _inproj(x, g, sc, sh, w, cos_t, sin_t, *, rope, tm):
    b, s, d = x.shape
    n = w.shape[-1]
    mod = pl.BlockSpec((1, 1, d), lambda i, t: (i, 0, 0))
    return pl.pallas_call(
        functools.partial(_inproj_kernel, rope=rope),
        grid=(b, s // tm),
        in_specs=[
            pl.BlockSpec((1, tm, d), lambda i, t: (i, t, 0)),
            pl.BlockSpec((1, d), lambda i, t: (0, 0)),
            mod, mod,
            pl.BlockSpec((d, n), lambda i, t: (0, 0)),
            pl.BlockSpec((tm, LANES), lambda i, t: (t, 0)),
            pl.BlockSpec((tm, LANES), lambda i, t: (t, 0)),
        ],
        out_specs=pl.BlockSpec((1, tm, n), lambda i, t: (i, t, 0)),
        out_shape=jax.ShapeDtypeStruct((b, s, n), BF),
        compiler_params=_cparams(("arbitrary", "arbitrary")),
        name="in_proj_rope" if rope else "in_proj_ctx",
    )(x, g, sc, sh, w, cos_t, sin_t)


def _rope_tables(seq):
    t = jnp.arange(seq)
    row = (t // GRID_W).astype(F32)
    col = (t % GRID_W).astype(F32)
    n_freq = ROPE_AXIS_DIM // 2
    inv_freq = ROPE_BASE ** (-jnp.arange(n_freq, dtype=F32) / n_freq)
    ang_r = row[:, None] * inv_freq[None, :]
    ang_c = col[:, None] * inv_freq[None, :]
    cos64 = jnp.concatenate([jnp.cos(ang_r), jnp.cos(ang_r), jnp.cos(ang_c), jnp.cos(ang_c)], axis=-1)
    sin64 = jnp.concatenate([-jnp.sin(ang_r), jnp.sin(ang_r), -jnp.sin(ang_c), jnp.sin(ang_c)], axis=-1)
    return jnp.tile(cos64, (1, 2)), jnp.tile(sin64, (1, 2))


def _diff_lambda(lam_ref, lam_init):
    lp = lam_ref[...]
    s1 = jnp.sum(lp[0:1] * lp[1:2], axis=-1, keepdims=True)
    s2 = jnp.sum(lp[2:3] * lp[3:4], axis=-1, keepdims=True)
    return jnp.exp(s1) - jnp.exp(s2) + lam_init


def _attend_phased(qms, kss, vss, biasess, s_ref, p_ref):
    n_chain = len(qms)
    offs = [np.cumsum([0] + [k.shape[0] for k in ks]) for ks in kss]
    for c in range(n_chain):
        for i, k in enumerate(kss[c]):
            s = _dot_nt(k, qms[c])
            if biasess[c][i] is not None:
                s = s + biasess[c][i]
            s_ref[c, offs[c][i]:offs[c][i + 1], :] = s
    dens = []
    for c in range(n_chain):
        s = s_ref[c]
        p = jnp.exp2(s - jnp.max(s, axis=0, keepdims=True))
        dens.append(jnp.sum(p, axis=0, keepdims=True))
        p_ref[c] = p.astype(BF)
    outs = []
    for c in range(n_chain):
        nums = [_dot_tn(v, p_ref[c, offs[c][i]:offs[c][i + 1], :]) for i, v in enumerate(vss[c])]
        outs.append(functools.reduce(jnp.add, nums) / dens[c])
    return outs


def _diff_head(q, ks, vs, lam, gsub, lam_init, s_ref, p_ref):
    lane = lax.broadcasted_iota(jnp.int32, q.shape, 1)
    zero = jnp.zeros_like(q)
    n_sub = q.shape[0] // s_ref.shape[2]
    q_maps = [jnp.where(lane < HEAD_DIM, q, zero), jnp.where(lane >= HEAD_DIM, q, zero)]
    qms = [qm[j * s_ref.shape[2]:(j + 1) * s_ref.shape[2]] for j in range(n_sub) for qm in q_maps]
    n_chain = len(qms)
    outs = _attend_phased(qms, [ks] * n_chain, [vs] * n_chain, [[None] * len(ks)] * n_chain, s_ref, p_ref)
    o = jnp.concatenate([(outs[2 * j] - lam * outs[2 * j + 1]).T for j in range(n_sub)], axis=0)
    return (_rms(o, SUBLN_EPS) * gsub) * (1.0 - lam_init)


def _diff_kernel(lam_ref, gs_ref, q_ref, *refs, n_kv, lam_init):
    k_refs, v_refs = refs[:n_kv], refs[n_kv:2 * n_kv]
    o_ref, s_ref, p_ref = refs[2 * n_kv:]
    lam = _diff_lambda(lam_ref, lam_init)
    o = _diff_head(q_ref[0], [r[0] for r in k_refs], [r[0] for r in v_refs], lam, gs_ref[...], lam_init,
                   s_ref, p_ref)
    o_ref[0] = o.astype(BF)


def _diff_attention(lam_p, gsub, q_src, kv_srcs, *, tq, lam_init, name):
    b, sq, _ = q_src.shape
    hw = 2 * HEAD_DIM
    k0, v0 = A_QK // hw, 2 * A_QK // hw
    n_keys = sum(s.shape[1] for s in kv_srcs)
    sub = min(tq, ATTN_Q_SUB)
    specs = [
        pl.BlockSpec(lam_p.shape, lambda i, h, t: (0, 0)),
        pl.BlockSpec((1, hw), lambda i, h, t: (0, 0)),
        pl.BlockSpec((1, tq, hw), lambda i, h, t: (i, t, h)),
    ]
    specs += [pl.BlockSpec((1, s.shape[1], hw), lambda i, h, t: (i, 0, k0 + h)) for s in kv_srcs]
    specs += [pl.BlockSpec((1, s.shape[1], hw), lambda i, h, t: (i, 0, v0 + h)) for s in kv_srcs]
    return pl.pallas_call(
        functools.partial(_diff_kernel, n_kv=len(kv_srcs), lam_init=lam_init),
        grid=(b, DIFF_HEADS, sq // tq),
        in_specs=specs,
        out_specs=pl.BlockSpec((1, tq, hw), lambda i, h, t: (i, t, h)),
        out_shape=jax.ShapeDtypeStruct((b, sq, A_V), BF),
        scratch_shapes=[pltpu.VMEM((2 * tq // sub, n_keys, sub), F32), pltpu.VMEM((2 * tq // sub, n_keys, sub), BF)],
        compiler_params=_cparams(("arbitrary", "arbitrary", "arbitrary")),
        name=name,
    )(lam_p, gsub, q_src, *kv_srcs, *kv_srcs)


def _na_kernel(q_ref, kl_ref, vl_ref, kc_ref, vc_ref, bias_ref, o_ref, s_ref, p_ref, *, rows):
    k = pl.program_id(1)
    start_row = jnp.clip(k * NA_QROWS - NA_KH // 2, 0, rows - NA_WIN_ROWS)
    nq = NA_QROWS * GRID_W
    win = NA_WIN_ROWS * GRID_W
    start = pl.multiple_of(start_row * GRID_W, nq)
    lane = lax.broadcasted_iota(jnp.int32, (nq, LANES), 1)
    row = lax.broadcasted_iota(jnp.int32, (LANES, nq), 0)
    qms, kss, vss, biasess = [], [], [], []
    for hp in range(NA_HEADS // 2):
        cs = slice(hp * LANES, (hp + 1) * LANES)
        q2 = q_ref[0, :, cs]
        zero = jnp.zeros_like(q2)
        qms += [jnp.where(lane < HEAD_DIM, q2, zero), jnp.where(lane >= HEAD_DIM, q2, zero)]
        kss += [[kl_ref[0, pl.ds(start, win), cs], kc_ref[0, :, cs]]] * 2
        vss += [[vl_ref[0, pl.ds(start, win), cs], vc_ref[0, :, cs]]] * 2
        biasess += [[bias_ref[0, 2 * hp], None], [bias_ref[0, 2 * hp + 1], None]]
    outs = _attend_phased(qms, kss, vss, biasess, s_ref, p_ref)
    for hp in range(NA_HEADS // 2):
        o_ref[0, :, hp * LANES:(hp + 1) * LANES] = jnp.where(row < HEAD_DIM, outs[2 * hp], outs[2 * hp + 1]).T.astype(BF)


def _na_bias_kind(k, n_steps):
    return jnp.where(k == 0, 0, jnp.where(k == n_steps - 1, 2, 1))


def _na_bias_table(rpb, rows):
    n_off = 2 * NA_KH - 1
    cq = np.arange(GRID_W)[:, None]
    ck = np.arange(GRID_W)[None, :]
    col_start = np.clip(cq - NA_KW // 2, 0, GRID_W - NA_KW)
    in_win = (ck >= col_start) & (ck < col_start + NA_KW)
    onehot = (ck - cq + (NA_KW - 1))[None] == np.arange(2 * NA_KW - 1)[:, None, None]
    tz = jnp.einsum('hrd,dqk->hrqk', rpb.astype(F32), jnp.asarray(onehot, F32), precision=lax.Precision.HIGHEST)
    tz = jnp.where(jnp.asarray(in_win)[None, None], tz, NEG_BIG)
    tz = jnp.concatenate([tz, jnp.full((NA_HEADS, 1, GRID_W, GRID_W), NEG_BIG, F32)], axis=1)
    n_steps = rows // NA_QROWS
    idx = np.full((3, NA_QROWS, NA_WIN_ROWS), n_off, np.int32)
    for kind, k in enumerate((0, 1, n_steps - 1)):
        start_row = min(max(k * NA_QROWS - NA_KH // 2, 0), rows - NA_WIN_ROWS)
        for q in range(NA_QROWS):
            r = k * NA_QROWS + q
            r_start = min(max(r - NA_KH // 2, 0), rows - NA_KH)
            for w in range(NA_WIN_ROWS):
                a = start_row + w
                if r_start <= a < r_start + NA_KH:
                    idx[kind, q, w] = a - r + (NA_KH - 1)
    t = jnp.take(tz, jnp.asarray(idx.reshape(-1)), axis=1)
    t = t.reshape(NA_HEADS, 3, NA_QROWS, NA_WIN_ROWS, GRID_W, GRID_W).transpose(1, 0, 3, 5, 2, 4)
    return t.reshape(3, NA_HEADS, NA_WIN_ROWS * GRID_W, NA_QROWS * GRID_W) * LOG2E


def _na_attention(p_lat, p_ctx, bias_tab, rows):
    b, s, _ = p_lat.shape
    lc = p_ctx.shape[1]
    base = (2 * A_QK + A_V) // B_QKV
    nq = NA_QROWS * GRID_W
    n_steps = rows // NA_QROWS
    return pl.pallas_call(
        functools.partial(_na_kernel, rows=rows),
        grid=(b, n_steps),
        in_specs=[
            pl.BlockSpec((1, nq, B_QKV), lambda i, k: (i, k, base)),
            pl.BlockSpec((1, s, B_QKV), lambda i, k: (i, 0, base + 1)),
            pl.BlockSpec((1, s, B_QKV), lambda i, k: (i, 0, base + 2)),
            pl.BlockSpec((1, lc, B_QKV), lambda i, k: (i, 0, base + 1)),
            pl.BlockSpec((1, lc, B_QKV), lambda i, k: (i, 0, base + 2)),
            pl.BlockSpec((1,) + bias_tab.shape[1:], lambda i, k: (_na_bias_kind(k, n_steps), 0, 0, 0)),
        ],
        out_specs=pl.BlockSpec((1, nq, B_QKV), lambda i, k: (i, k, 0)),
        out_shape=jax.ShapeDtypeStruct((b, s, B_QKV), BF),
        scratch_shapes=[pltpu.VMEM((NA_HEADS, NA_WIN_ROWS * GRID_W + lc, nq), F32),
                        pltpu.VMEM((NA_HEADS, NA_WIN_ROWS * GRID_W + lc, nq), BF)],
        compiler_params=_cparams(("arbitrary", "arbitrary")),
        name="na_attn",
    )(p_lat, p_lat, p_lat, p_ctx, p_ctx, bias_tab)


def _dense_ctx_kernel(q_ref, k_ref, v_ref, o_ref, s_ref, p_ref):
    lane = lax.broadcasted_iota(jnp.int32, (q_ref.shape[1], LANES), 1)
    row = lax.broadcasted_iota(jnp.int32, (LANES, q_ref.shape[1]), 0)
    qms, kss, vss = [], [], []
    for hp in range(NA_HEADS // 2):
        cs = slice(hp * LANES, (hp + 1) * LANES)
        q2 = q_ref[0, :, cs]
        zero = jnp.zeros_like(q2)
        qms += [jnp.where(lane < HEAD_DIM, q2, zero), jnp.where(lane >= HEAD_DIM, q2, zero)]
        kss += [[k_ref[0, :, cs]]] * 2
        vss += [[v_ref[0, :, cs]]] * 2
    outs = _attend_phased(qms, kss, vss, [[None]] * NA_HEADS, s_ref, p_ref)
    for hp in range(NA_HEADS // 2):
        o_ref[0, :, hp * LANES:(hp + 1) * LANES] = jnp.where(row < HEAD_DIM, outs[2 * hp], outs[2 * hp + 1]).T.astype(BF)


def _dense_ctx_attention(p_ctx):
    b, lc, _ = p_ctx.shape
    base = (2 * A_QK + A_V) // B_QKV
    spec = lambda j: pl.BlockSpec((1, lc, B_QKV), lambda i: (i, 0, base + j))
    return pl.pallas_call(
        _dense_ctx_kernel,
        grid=(b,),
        in_specs=[spec(0), spec(1), spec(2)],
        out_specs=pl.BlockSpec((1, lc, B_QKV), lambda i: (i, 0, 0)),
        out_shape=jax.ShapeDtypeStruct((b, lc, B_QKV), BF),
        scratch_shapes=[pltpu.VMEM((NA_HEADS, lc, lc), F32), pltpu.VMEM((NA_HEADS, lc, lc), BF)],
        compiler_params=_cparams(("arbitrary",)),
        name="dense_ctx_attn",
    )(p_ctx, p_ctx, p_ctx)


def _mlp_residual(x1, n2g, sc2, sh2, g2, w1_ref, w2_ref):
    h2 = _modulated_norm(x1, n2g, sc2, sh2).astype(BF)
    hc = 1024
    acc = jnp.zeros_like(x1)
    for c in range(MLP_HIDDEN // hc):
        hid = jnp.maximum(_dot(h2, w1_ref[:, c * hc:(c + 1) * hc]), 0.0)
        acc = acc + _dot((hid * hid).astype(BF), w2_ref[c * hc:(c + 1) * hc, :])
    return x1 + g2 * acc


def _even_tail_kernel(oa_ref, ob_ref, x_ref, wo_ref, w1_ref, w2_ref, n2g_ref, g1_ref, sc2_ref, sh2_ref, g2_ref,
                      n1g_ref, sc1n_ref, sh1n_ref, xo_ref, hn_ref):
    y = _dot(oa_ref[0], wo_ref[0:A_V, :]) + _dot(ob_ref[0], wo_ref[A_V:, :])
    x1 = x_ref[0] + g1_ref[0] * y
    x2 = _mlp_residual(x1, n2g_ref[...], sc2_ref[0], sh2_ref[0], g2_ref[0], w1_ref, w2_ref)
    xo_ref[0] = x2
    hn_ref[0] = _modulated_norm(x2, n1g_ref[...], sc1n_ref[0], sh1n_ref[0])


def _even_tail(oa, ob, x, wo, w1, w2, n2g, g1, sc2, sh2, g2, n1g, sc1n, sh1n, *, tm):
    b, s, d = x.shape
    tok = lambda w: pl.BlockSpec((1, tm, w), lambda i, t: (i, t, 0))
    mod = pl.BlockSpec((1, 1, d), lambda i, t: (i, 0, 0))
    vec = pl.BlockSpec((1, d), lambda i, t: (0, 0))
    full = lambda a: pl.BlockSpec(a.shape, lambda i, t: (0, 0))
    return pl.pallas_call(
        _even_tail_kernel,
        grid=(b, s // tm),
        in_specs=[tok(A_V), tok(B_QKV), tok(d), full(wo), full(w1), full(w2), vec, mod, mod, mod, mod, vec, mod, mod],
        out_specs=[tok(d), tok(d)],
        out_shape=[jax.ShapeDtypeStruct((b, s, d), F32), jax.ShapeDtypeStruct((b, s, d), F32)],
        compiler_params=_cparams(("arbitrary", "arbitrary")),
        name="even_tail",
    )(oa, ob, x, wo, w1, w2, n2g, g1, sc2, sh2, g2, n1g, sc1n, sh1n)


def _odd_tail_kernel(y_ref, h_ref, x_ref, dsk_ref, wa_ref, wb_ref, w1_ref, w2_ref, n2g_ref, g1_ref, sc2_ref, sh2_ref,
                     g2_ref, fg_ref, o_ref):
    z = y_ref[0] + dsk_ref[...] * h_ref[0]
    z = jax.nn.gelu(z).astype(BF)
    mix = _dot(z, wa_ref[...]) * jax.nn.sigmoid(_dot(z, wb_ref[...]))
    x1 = x_ref[0] + g1_ref[0] * mix
    x2 = _mlp_residual(x1, n2g_ref[...], sc2_ref[0], sh2_ref[0], g2_ref[0], w1_ref, w2_ref)
    o_ref[0] = _rms(x2, NORM_EPS) * fg_ref[...]


def _odd_tail(y, h, x, dsk, wa, wb, w1, w2, n2g, g1, sc2, sh2, g2, fg, *, tm):
    b, s, d = x.shape
    tok = pl.BlockSpec((1, tm, d), lambda i, t: (i, t, 0))
    mod = pl.BlockSpec((1, 1, d), lambda i, t: (i, 0, 0))
    vec = pl.BlockSpec((1, d), lambda i, t: (0, 0))
    full = lambda a: pl.BlockSpec(a.shape, lambda i, t: (0, 0))
    return pl.pallas_call(
        _odd_tail_kernel,
        grid=(b, s // tm),
        in_specs=[tok, tok, tok, vec, full(wa), full(wb), full(w1), full(w2), vec, mod, mod, mod, mod, vec],
        out_specs=tok,
        out_shape=jax.ShapeDtypeStruct((b, s, d), F32),
        compiler_params=_cparams(("arbitrary", "arbitrary")),
        name="odd_tail",
    )(y, h, x, dsk, wa, wb, w1, w2, n2g, g1, sc2, sh2, g2, fg)


def _slot_tables():
    q = np.arange(SSM_GROUPS)
    m = q % SUBLANES
    group = (q // SUBLANES) * SUBLANES + (SUBLANES - m) % SUBLANES
    pos = np.arange(SSM_CHUNK)[None, :]
    octet = (pos // SUBLANES) * SUBLANES
    tok_in = octet + (pos % SUBLANES + m[:, None]) % SUBLANES
    tok_out = octet + ((SUBLANES - m[:, None]) % SUBLANES - pos % SUBLANES) % SUBLANES
    return group, tok_in, tok_out


def _s5_operators(lam_re, lam_im, log_step, b_re, b_im, c_re, c_im):
    t_len = SSM_CHUNK
    hp = lax.Precision.HIGHEST
    group, tok_in, tok_out = _slot_tables()
    g = SSM_GROUPS
    lr = lam_re.astype(F32)[:, group]
    li = lam_im.astype(F32)[:, group]
    delta = jnp.exp(log_step.astype(F32))[:, group, None]
    dec = lr * delta
    ang = li * delta
    mag1 = jnp.exp(dec)
    ar1 = mag1 * jnp.cos(ang)
    ai1 = mag1 * jnp.sin(ang)
    den = lr * lr + li * li
    nr = ar1 - 1.0
    cr = (nr * lr + ai1 * li) / den
    ci = (ai1 * lr - nr * li) / den
    br = b_re.astype(F32)[:, group]
    bi = b_im.astype(F32)[:, group]
    bbr = (cr[..., None] * br - ci[..., None] * bi).transpose(0, 1, 3, 2)
    bbi = (cr[..., None] * bi + ci[..., None] * br).transpose(0, 1, 3, 2)
    cre = c_re.astype(F32)[:, group]
    cim = c_im.astype(F32)[:, group]

    def apow(d, e):
        e = jnp.asarray(e, F32)[:, :, None]
        mag = jnp.exp(dec[d][:, None, :] * e)
        return mag * jnp.cos(ang[d][:, None, :] * e), mag * jnp.sin(ang[d][:, None, :] * e)

    def in_factor(d, e):
        ar, ai = apow(d, e)
        re = ar[:, :, None, :] * bbr[d][:, None] - ai[:, :, None, :] * bbi[d][:, None]
        im = ar[:, :, None, :] * bbi[d][:, None] + ai[:, :, None, :] * bbr[d][:, None]
        return re.reshape(g, SSM_CW, SSM_STATE), im.reshape(g, SSM_CW, SSM_STATE)

    def out_factor(d, e):
        ar, ai = apow(d, e)
        ar = ar.transpose(0, 2, 1)[:, :, :, None]
        ai = ai.transpose(0, 2, 1)[:, :, :, None]
        ct_r = cre[d].transpose(0, 2, 1)[:, :, None, :]
        ct_i = cim[d].transpose(0, 2, 1)[:, :, None, :]
        re = ar * ct_r - ai * ct_i
        im = ar * ct_i + ai * ct_r
        return re.reshape(g, SSM_STATE, SSM_CW), (-im).reshape(g, SSM_STATE, SSM_CW)

    def toeplitz(d, e_in, e_out, valid):
        lre, lim = in_factor(d, e_in)
        rre, rim = out_factor(d, e_out)
        full = jnp.einsum('gxp,gpy->gxy', jnp.concatenate([lre, lim], axis=-1), jnp.concatenate([rre, rim], axis=1),
                          precision=hp)
        mask = np.broadcast_to(valid[:, :, None, :, None], (g, t_len, SSM_GROUP, t_len, SSM_GROUP))
        return jnp.where(jnp.asarray(mask.reshape(g, SSM_CW, SSM_CW)), full, 0.0)

    s_in = tok_in[:, :, None]
    t_out = tok_out[:, None, :]
    toep = toeplitz(0, -tok_in, tok_out, t_out >= s_in) + toeplitz(1, tok_in, -tok_out, s_in >= t_out)

    w_in = jnp.stack(in_factor(0, t_len - 1 - tok_in) + in_factor(1, tok_in), axis=2)
    w_st = jnp.stack(out_factor(0, tok_out + 1) + out_factor(1, t_len - tok_out), axis=1)
    full_t = np.full((g, 1), t_len)
    a_t = jnp.concatenate(apow(0, full_t) + apow(1, full_t), axis=1)

    eye = jnp.eye(2, dtype=F32)
    w_in_p = w_in.reshape(g // 2, 2, SSM_CW, 4, 1, SSM_STATE) * eye[None, :, None, None, :, None]
    w_in_p = w_in_p.reshape(g, SSM_CW, 8 * SSM_STATE)
    w_st_p = w_st.reshape(g // 2, 2, 4, 1, SSM_STATE, SSM_CW) * eye[None, :, None, :, None, None]
    w_st_p = w_st_p.reshape(g, 8 * SSM_STATE, SSM_CW)
    a_t_p = a_t.reshape(g // 2, 2, 4, SSM_STATE).transpose(0, 2, 1, 3).reshape(g // 2, 4, 2 * SSM_STATE)
    return toep.astype(BF), w_in_p.astype(BF), w_st_p.astype(BF), a_t_p


def _to_slots(x):
    a = pltpu.roll(x, 0, 1, stride=SSM_GROUP, stride_axis=0)
    a = a.reshape(x.shape[0] // SUBLANES, SUBLANES, LANES)
    blk = lax.broadcasted_iota(jnp.int32, a.shape, 2) // SSM_GROUP
    for bit in (1, 2, 4):
        a = jnp.where((blk & bit) != 0, pltpu.roll(a, SUBLANES - bit, 1), a)
    return a


def _from_slots(t):
    blk = lax.broadcasted_iota(jnp.int32, t.shape, 2) // SSM_GROUP
    for bit in (1, 2, 4):
        t = jnp.where((blk & bit) != 0, pltpu.roll(t, SUBLANES - bit, 1), t)
    a = t.reshape(t.shape[0] * SUBLANES, LANES)
    return pltpu.roll(a, 0, 1, stride=SSM_GROUP, stride_axis=0)


def _s5_pack_kernel(h_ref, hc_ref, o_ref, *, n_ctx, n_lat):
    def body(b, carry):
        tc = _to_slots(hc_ref[b]).reshape(n_ctx, 2, SUBLANES, LANES)
        tl = _to_slots(h_ref[b]).reshape(n_lat, 2, SUBLANES, LANES)
        for o in range(2):
            o_ref[0, o, 0:n_ctx, b] = tc[:, o]
            o_ref[0, o, n_ctx:, b] = tl[:, o]
        return carry

    lax.fori_loop(0, h_ref.shape[0], body, 0)


def _s5_pack(h, hc):
    b, s, d = h.shape
    lc = hc.shape[1]
    n_ctx, n_lat = lc // SSM_CHUNK, s // SSM_CHUNK
    shape = (d // LANES, 2, n_ctx + n_lat, b, SUBLANES, LANES)
    return pl.pallas_call(
        functools.partial(_s5_pack_kernel, n_ctx=n_ctx, n_lat=n_lat),
        grid=(d // LANES,),
        in_specs=[pl.BlockSpec((b, s, LANES), lambda j: (0, 0, j)), pl.BlockSpec((b, lc, LANES), lambda j: (0, 0, j))],
        out_specs=pl.BlockSpec((1,) + shape[1:], lambda j: (j, 0, 0, 0, 0, 0)),
        out_shape=jax.ShapeDtypeStruct(shape, F32),
        compiler_params=_cparams(("arbitrary",)),
        name="s5_pack",
    )(h, hc)


def _s5_unpack_kernel(y_ref, o_ref, *, n_lat):
    def body(b, carry):
        t = jnp.stack([y_ref[0, 0, :, b], y_ref[0, 1, :, b]], axis=1).reshape(n_lat * 2, SUBLANES, LANES)
        o_ref[b] = _from_slots(t)
        return carry

    lax.fori_loop(0, o_ref.shape[0], body, 0)


def _s5_unpack(y, b):
    nj, _, n_lat = y.shape[:3]
    s = n_lat * SSM_CHUNK
    return pl.pallas_call(
        functools.partial(_s5_unpack_kernel, n_lat=n_lat),
        grid=(nj,),
        in_specs=[pl.BlockSpec((1,) + y.shape[1:], lambda j: (j, 0, 0, 0, 0, 0))],
        out_specs=pl.BlockSpec((b, s, LANES), lambda j: (0, 0, j)),
        out_shape=jax.ShapeDtypeStruct((b, s, nj * LANES), F32),
        compiler_params=_cparams(("arbitrary",)),
        name="s5_unpack",
    )(y)


def _s5_kernel(u_ref, toep_ref, win_ref, wst_ref, at_ref, y_ref, sloc_ref, sin_ref, *, n_ctx, n_lat, nb):
    p = pl.program_id(1)
    w = 2 * SSM_STATE
    n_rows = (n_ctx + n_lat) * nb
    lat0 = n_ctx * nb
    slots = [2 * p, 2 * p + 1]
    us = [jnp.concatenate([u_ref[0, o, pl.ds(m, n_rows, stride=SUBLANES), :] for o in range(2)], axis=-1).astype(BF)
          for m in slots]
    sloc_ref[...] = _dot(us[0], win_ref[0]) + _dot(us[1], win_ref[1])

    afr = jnp.broadcast_to(at_ref[0, 0:1, :], (nb, w))
    afi = jnp.broadcast_to(at_ref[0, 1:2, :], (nb, w))
    abr = jnp.broadcast_to(at_ref[0, 2:3, :], (nb, w))
    abi = jnp.broadcast_to(at_ref[0, 3:4, :], (nb, w))

    def advance(carry, rf, rb):
        fr, fi, br, bi = carry
        xfr = sloc_ref[pl.ds(rf, nb), 0:w]
        xfi = sloc_ref[pl.ds(rf, nb), w:2 * w]
        xbr = sloc_ref[pl.ds(rb, nb), 2 * w:3 * w]
        xbi = sloc_ref[pl.ds(rb, nb), 3 * w:4 * w]
        return (afr * fr - afi * fi + xfr, afr * fi + afi * fr + xfi,
                abr * br - abi * bi + xbr, abr * bi + abi * br + xbi)

    def ctx_body(i, carry):
        rf = pl.multiple_of(i * nb, nb)
        rb = pl.multiple_of((n_ctx - 1 - i) * nb, nb)
        return advance(carry, rf, rb)

    def lat_body(i, carry):
        fr, fi, br, bi = carry
        rf = pl.multiple_of(i * nb, nb)
        rb = pl.multiple_of((n_lat - 1 - i) * nb, nb)
        sin_ref[pl.ds(rf, nb), 0:w] = fr
        sin_ref[pl.ds(rf, nb), w:2 * w] = fi
        sin_ref[pl.ds(rb, nb), 2 * w:3 * w] = br
        sin_ref[pl.ds(rb, nb), 3 * w:4 * w] = bi
        return advance(carry, lat0 + rf, lat0 + rb)

    zero = jnp.zeros((nb, w), F32)
    carry = lax.fori_loop(0, n_ctx, ctx_body, (zero, zero, zero, zero))
    lax.fori_loop(0, n_lat, lat_body, carry)

    s_in = sin_ref[...].astype(BF)
    for q in range(2):
        y = _dot(us[q][lat0:], toep_ref[q]) + _dot(s_in, wst_ref[q])
        m_out = (SUBLANES - slots[q]) % SUBLANES
        for o in range(2):
            y_ref[0, o, pl.ds(m_out, n_lat * nb, stride=SUBLANES), :] = y[:, o * LANES:(o + 1) * LANES]


def _s5_scan(u, toep, w_in, w_st, a_t, *, n_ctx, nb):
    nj, _, r8, _ = u.shape
    n_lat = r8 // (nb * SUBLANES) - n_ctx
    cw = SSM_CW
    n_pairs = SUBLANES // 2
    pair = lambda r, c: pl.BlockSpec((2, r, c), lambda j, p: (j * n_pairs + p, 0, 0))
    return pl.pallas_call(
        functools.partial(_s5_kernel, n_ctx=n_ctx, n_lat=n_lat, nb=nb),
        grid=(nj, n_pairs),
        in_specs=[pl.BlockSpec((1, 2, r8, LANES), lambda j, p: (j, 0, 0, 0)),
                  pair(cw, cw), pair(cw, 8 * SSM_STATE), pair(8 * SSM_STATE, cw),
                  pl.BlockSpec((1, 4, 2 * SSM_STATE), lambda j, p: (j * n_pairs + p, 0, 0))],
        out_specs=pl.BlockSpec((1, 2, n_lat * nb * SUBLANES, LANES), lambda j, p: (j, 0, 0, 0)),
        out_shape=jax.ShapeDtypeStruct((nj, 2, n_lat * nb * SUBLANES, LANES), F32),
        scratch_shapes=[pltpu.VMEM(((n_ctx + n_lat) * nb, 8 * SSM_STATE), F32),
                        pltpu.VMEM((n_lat * nb, 8 * SSM_STATE), F32)],
        compiler_params=_cparams(("arbitrary", "arbitrary")),
        name="s5_scan",
    )(u, toep, w_in, w_st, a_t)


def _s5_mix(h, hc, params):
    b, s, d = h.shape
    n_ctx = hc.shape[1] // SSM_CHUNK
    n_lat = s // SSM_CHUNK
    toep, w_si, w_st, a_t = _s5_operators(*params)
    u = _s5_pack(h, hc).reshape(d // LANES, 2, (n_ctx + n_lat) * b * SUBLANES, LANES)
    y = _s5_scan(u, toep, w_si, w_st, a_t, n_ctx=n_ctx, nb=b)
    return _s5_unpack(y.reshape(d // LANES, 2, n_lat, b, SUBLANES, LANES), b)


def kernel(x, c, ctx, c_ctx, w_ada, b_ada, norm1_g, norm2_g, final_g, w_in, w_out, lam_q1, lam_k1, lam_q2, lam_k2,
           subln_g, na_rpb, ssm_lam_re, ssm_lam_im, ssm_log_step, ssm_b_re, ssm_b_im, ssm_c_re, ssm_c_im, ssm_d,
           glu_w_a, glu_w_b, mlp_w1, mlp_w2):
    b, s, d = x.shape
    lc = ctx.shape[1]
    rows = s // GRID_W
    assert DEPTH == 2 and d == D_MODEL and s % GRID_W == 0 and rows >= NA_WIN_ROWS and rows % NA_QROWS == 0
    assert s % SSM_CHUNK == 0 and lc % SSM_CHUNK == 0 and b % SUBLANES == 0

    n_act = -(-(b + 1) // SUBLANES) * SUBLANES
    act_in = jnp.zeros((n_act, d), F32).at[:b].set(c).at[b].set(c_ctx)
    m_all = _ada(act_in, w_ada, b_ada)

    def mods(i):
        lat = [m_all[i, :b, j * d:(j + 1) * d].reshape(b, 1, d) for j in range(6)]
        cx = [jnp.broadcast_to(m_all[i, b, j * d:(j + 1) * d].reshape(1, 1, d), (b, 1, d)) for j in range(6)]
        return lat, cx

    (sh1, sc1, g1, sh2, sc2, g2), (sh1c, sc1c, g1c, sh2c, sc2c, g2c) = mods(0)
    (sh1n, sc1n, g1n, sh2n, sc2n, g2n), (sh1nc, sc1nc, _, _, _, _) = mods(1)
    n1g = [norm1_g[i].reshape(1, d) for i in range(DEPTH)]
    n2g = [norm2_g[i].reshape(1, d) for i in range(DEPTH)]

    cos_t, sin_t = _rope_tables(s)
    w_in_b = w_in[0].astype(BF)
    p_lat = _inproj(x, n1g[0], sc1, sh1, w_in_b, cos_t, sin_t, rope=True, tm=1024)
    p_ctx = _inproj(ctx, n1g[0], sc1c, sh1c, w_in_b, cos_t[:lc], sin_t[:lc], rope=False, tm=lc)

    lam_init = 0.8 - 0.6 * math.exp(-0.3 * 0)
    lam_p = jnp.stack([lam_q1[0], lam_k1[0], lam_q2[0], lam_k2[0]]).astype(F32)
    gsub = subln_g[0].reshape(1, 2 * HEAD_DIM).astype(F32)
    oa = _diff_attention(lam_p, gsub, p_lat, [p_lat, p_ctx], tq=512, lam_init=lam_init, name="diff_attn")
    oac = _diff_attention(lam_p, gsub, p_ctx, [p_ctx], tq=lc, lam_init=lam_init, name="diff_attn_ctx")
    ob = _na_attention(p_lat, p_ctx, _na_bias_table(na_rpb[0], rows), rows)
    obc = _dense_ctx_attention(p_ctx)

    wo_b = w_out[0].astype(BF)
    w1_b = [mlp_w1[i].astype(BF) for i in range(DEPTH)]
    w2_b = [mlp_w2[i].astype(BF) for i in range(DEPTH)]
    x1, h1 = _even_tail(oa, ob, x, wo_b, w1_b[0], w2_b[0], n2g[0], g1, sc2, sh2, g2, n1g[1], sc1n, sh1n, tm=512)
    _, hc1 = _even_tail(oac, obc, ctx, wo_b, w1_b[0], w2_b[0], n2g[0], g1c, sc2c, sh2c, g2c, n1g[1], sc1nc, sh1nc,
                        tm=lc)

    y = _s5_mix(h1, hc1, (ssm_lam_re[0], ssm_lam_im[0], ssm_log_step[0], ssm_b_re[0], ssm_b_im[0], ssm_c_re[0],
                          ssm_c_im[0]))
    return _odd_tail(y, h1, x1, ssm_d[0].reshape(1, d).astype(F32), glu_w_a[0].astype(BF), glu_w_b[0].astype(BF),
                     w1_b[1], w2_b[1], n2g[1], g1n, sc2n, sh2n, g2n, final_g.reshape(1, d).astype(F32), tm=512)
```

```python
import functools
import math

import jax
import jax.numpy as jnp
import numpy as np
from jax import lax
from jax.experimental import pallas as pl
from jax.experimental.pallas import tpu as pltpu

D_MODEL = 1024
DEPTH = 2
GRID_W = 64
HEAD_DIM = 64
DIFF_HEADS = 4
NA_HEADS = 8
NA_KH = 8
NA_KW = 16
NA_QROWS = 4
NA_WIN_ROWS = 12
ATTN_Q_SUB = 256
ROPE_BASE = 10000.0
ROPE_AXIS_DIM = HEAD_DIM // 2
SSM_GROUP = 16
SSM_GROUPS = D_MODEL // SSM_GROUP
SSM_STATE = 64
MLP_HIDDEN = 4 * D_MODEL
A_QK = DIFF_HEADS * 2 * HEAD_DIM
A_V = DIFF_HEADS * 2 * HEAD_DIM
B_QKV = NA_HEADS * HEAD_DIM
IN_PROJ = 2 * A_QK + A_V + 3 * B_QKV
NORM_EPS = 1e-6
SUBLN_EPS = 1e-5

LANES = 128
SUBLANES = 8
SSM_CHUNK = 16
SSM_CW = SSM_CHUNK * SSM_GROUP
NEG_BIG = -1e30
LOG2E = math.log2(math.e)
VMEM_LIMIT = 56 * 1024 * 1024

BF = jnp.bfloat16
F32 = jnp.float32


def _cparams(sem):
    return pltpu.CompilerParams(dimension_semantics=sem, vmem_limit_bytes=VMEM_LIMIT)


def _dot(a, b):
    return jnp.dot(a, b, preferred_element_type=F32)


def _dot_nt(a, b):
    return lax.dot_general(a, b, (((1,), (1,)), ((), ())), preferred_element_type=F32)


def _dot_tn(a, b):
    return lax.dot_general(a, b, (((0,), (0,)), ((), ())), preferred_element_type=F32)


def _rms(x, eps):
    return x * lax.rsqrt(jnp.mean(x * x, axis=-1, keepdims=True) + eps)


def _ada_kernel(a_ref, w_ref, b_ref, o_ref):
    a = a_ref[...]
    act = a * jax.nn.sigmoid(a)
    o_ref[0] = _dot(act.astype(BF), w_ref[0].astype(BF)) + b_ref[0]


def _ada(act_in, w_ada, b_ada):
    r, d = act_in.shape
    n = w_ada.shape[-1]
    tn = 1536
    return pl.pallas_call(
        _ada_kernel,
        grid=(DEPTH, n // tn),
        in_specs=[
            pl.BlockSpec((r, d), lambda i, j: (0, 0)),
            pl.BlockSpec((1, d, tn), lambda i, j: (i, 0, j)),
            pl.BlockSpec((1, 1, tn), lambda i, j: (i, 0, j)),
        ],
        out_specs=pl.BlockSpec((1, r, tn), lambda i, j: (i, 0, j)),
        out_shape=jax.ShapeDtypeStruct((DEPTH, r, n), F32),
        compiler_params=_cparams(("arbitrary", "arbitrary")),
        name="ada_ln",
    )(act_in, w_ada, b_ada.reshape(DEPTH, 1, n))


def _modulated_norm(x, g, sc, sh):
    return (_rms(x, NORM_EPS) * g) * (1.0 + sc) + sh


def _rope_chunk(x, cos, sin, low):
    up = pltpu.roll(x, LANES - 16, 1)
    dn = pltpu.roll(x, 16, 1)
    return x * cos + jnp.where(low, up, dn) * sin


def _inproj_kernel(x_ref, g_ref, sc_ref, sh_ref, w_ref, cos_ref, sin_ref, o_ref, *, rope):
    h = _modulated_norm(x_ref[0], g_ref[...], sc_ref[0, 0], sh_ref[0, 0]).astype(BF)
    tm = h.shape[0]
    if rope:
        cos = cos_ref[...]
        sin = sin_ref[...]
        low = (lax.broadcasted_iota(jnp.int32, (tm, LANES), 1) % 32) < 16
    cw = 512
    for j in range(IN_PROJ // cw):
        acc = _dot(h, w_ref[:, j * cw:(j + 1) * cw])
        if j in (0, 3):
            acc = acc * (HEAD_DIM ** -0.5 * LOG2E)
        if rope and j < 2:
            parts = [_rope_chunk(acc[:, c * LANES:(c + 1) * LANES], cos, sin, low) for c in range(cw // LANES)]
            acc = jnp.concatenate(parts, axis=-1)
        o_ref[0, :, j * cw:(j + 1) * cw] = acc.astype(BF)


def _mod_spec(d, layer, col, row):
    if row is None:
        return pl.BlockSpec((1, 1, 1, d), lambda i, t: (layer, i, 0, col))
    return pl.BlockSpec((1, 1, 1, d), lambda i, t: (layer, row, 0, col))


MOD_SH1, MOD_SC1, MOD_G1, MOD_SH2, MOD_SC2, MOD_G2 = range(6)


def _inproj(x, g, m4, layer, row, w, cos_t, sin_t, *, rope, tm):
    b, s, d = x.shape
    n = w.shape[-1]
    return pl.pallas_call(
        functools.partial(_inproj_kernel, rope=rope),
        grid=(b, s // tm),
        in_specs=[
            pl.BlockSpec((1, tm, d), lambda i, t: (i, t, 0)),
            pl.BlockSpec((1, d), lambda i, t: (0, 0)),
            _mod_spec(d, layer, MOD_SC1, row), _mod_spec(d, layer, MOD_SH1, row),
            pl.BlockSpec((d, n), lambda i, t: (0, 0)),
            pl.BlockSpec((tm, LANES), lambda i, t: (t, 0)),
            pl.BlockSpec((tm, LANES), lambda i, t: (t, 0)),
        ],
        out_specs=pl.BlockSpec((1, tm, n), lambda i, t: (i, t, 0)),
        out_shape=jax.ShapeDtypeStruct((b, s, n), BF),
        compiler_params=_cparams(("arbitrary", "arbitrary")),
        name="in_proj_rope" if rope else "in_proj_ctx",
    )(x, g, m4, m4, w, cos_t, sin_t)


def _rope_tables(seq):
    t = jnp.arange(seq)
    row = (t // GRID_W).astype(F32)
    col = (t % GRID_W).astype(F32)
    n_freq = ROPE_AXIS_DIM // 2
    inv_freq = ROPE_BASE ** (-jnp.arange(n_freq, dtype=F32) / n_freq)
    ang_r = row[:, None] * inv_freq[None, :]
    ang_c = col[:, None] * inv_freq[None, :]
    cos64 = jnp.concatenate([jnp.cos(ang_r), jnp.cos(ang_r), jnp.cos(ang_c), jnp.cos(ang_c)], axis=-1)
    sin64 = jnp.concatenate([-jnp.sin(ang_r), jnp.sin(ang_r), -jnp.sin(ang_c), jnp.sin(ang_c)], axis=-1)
    return jnp.tile(cos64, (1, 2)), jnp.tile(sin64, (1, 2))


def _diff_lambda(lam_ref, lam_init):
    lp = lam_ref[...]
    s1 = jnp.sum(lp[0:1] * lp[1:2], axis=-1, keepdims=True)
    s2 = jnp.sum(lp[2:3] * lp[3:4], axis=-1, keepdims=True)
    return jnp.exp(s1) - jnp.exp(s2) + lam_init


def _attend_phased(qms, kss, vss, biasess, s_ref, p_ref):
    n_chain = len(qms)
    offs = [np.cumsum([0] + [k.shape[0] for k in ks]) for ks in kss]
    for c in range(n_chain):
        for i, k in enumerate(kss[c]):
            s = _dot_nt(k, qms[c])
            if biasess[c][i] is not None:
                s = s + biasess[c][i]
            s_ref[c, offs[c][i]:offs[c][i + 1], :] = s
    dens = []
    for c in range(n_chain):
        s = s_ref[c]
        p = jnp.exp2(s - jnp.max(s, axis=0, keepdims=True))
        dens.append(jnp.sum(p, axis=0, keepdims=True))
        p_ref[c] = p.astype(BF)
    outs = []
    for c in range(n_chain):
        nums = [_dot_tn(v, p_ref[c, offs[c][i]:offs[c][i + 1], :]) for i, v in enumerate(vss[c])]
        outs.append(functools.reduce(jnp.add, nums) / dens[c])
    return outs


def _diff_head(q, ks, vs, lam, gsub, lam_init, s_ref, p_ref):
    lane = lax.broadcasted_iota(jnp.int32, q.shape, 1)
    zero = jnp.zeros_like(q)
    n_sub = q.shape[0] // s_ref.shape[2]
    q_maps = [jnp.where(lane < HEAD_DIM, q, zero), jnp.where(lane >= HEAD_DIM, q, zero)]
    qms = [qm[j * s_ref.shape[2]:(j + 1) * s_ref.shape[2]] for j in range(n_sub) for qm in q_maps]
    n_chain = len(qms)
    outs = _attend_phased(qms, [ks] * n_chain, [vs] * n_chain, [[None] * len(ks)] * n_chain, s_ref, p_ref)
    o = jnp.concatenate([(outs[2 * j] - lam * outs[2 * j + 1]).T for j in range(n_sub)], axis=0)
    return (_rms(o, SUBLN_EPS) * gsub) * (1.0 - lam_init)


def _diff_kernel(lam_ref, gs_ref, q_ref, *refs, n_kv, lam_init):
    k_refs, v_refs = refs[:n_kv], refs[n_kv:2 * n_kv]
    o_ref, s_ref, p_ref = refs[2 * n_kv:]
    lam = _diff_lambda(lam_ref, lam_init)
    o = _diff_head(q_ref[0], [r[0] for r in k_refs], [r[0] for r in v_refs], lam, gs_ref[...], lam_init,
                   s_ref, p_ref)
    o_ref[0] = o.astype(BF)


def _diff_attention(lam_p, gsub, q_src, kv_srcs, *, tq, lam_init, name):
    b, sq, _ = q_src.shape
    hw = 2 * HEAD_DIM
    k0, v0 = A_QK // hw, 2 * A_QK // hw
    n_keys = sum(s.shape[1] for s in kv_srcs)
    sub = min(tq, ATTN_Q_SUB)
    specs = [
        pl.BlockSpec(lam_p.shape, lambda i, h, t: (0, 0)),
        pl.BlockSpec((1, hw), lambda i, h, t: (0, 0)),
        pl.BlockSpec((1, tq, hw), lambda i, h, t: (i, t, h)),
    ]
    specs += [pl.BlockSpec((1, s.shape[1], hw), lambda i, h, t: (i, 0, k0 + h)) for s in kv_srcs]
    specs += [pl.BlockSpec((1, s.shape[1], hw), lambda i, h, t: (i, 0, v0 + h)) for s in kv_srcs]
    return pl.pallas_call(
        functools.partial(_diff_kernel, n_kv=len(kv_srcs), lam_init=lam_init),
        grid=(b, DIFF_HEADS, sq // tq),
        in_specs=specs,
        out_specs=pl.BlockSpec((1, tq, hw), lambda i, h, t: (i, t, h)),
        out_shape=jax.ShapeDtypeStruct((b, sq, A_V), BF),
        scratch_shapes=[pltpu.VMEM((2 * tq // sub, n_keys, sub), F32), pltpu.VMEM((2 * tq // sub, n_keys, sub), BF)],
        compiler_params=_cparams(("arbitrary", "arbitrary", "arbitrary")),
        name=name,
    )(lam_p, gsub, q_src, *kv_srcs, *kv_srcs)


def _na_kernel(q_ref, kl_ref, vl_ref, kc_ref, vc_ref, bias_ref, o_ref, s_ref, p_ref, *, rows):
    k = pl.program_id(1)
    start_row = jnp.clip(k * NA_QROWS - NA_KH // 2, 0, rows - NA_WIN_ROWS)
    nq = NA_QROWS * GRID_W
    win = NA_WIN_ROWS * GRID_W
    start = pl.multiple_of(start_row * GRID_W, nq)
    lane = lax.broadcasted_iota(jnp.int32, (nq, LANES), 1)
    row = lax.broadcasted_iota(jnp.int32, (LANES, nq), 0)
    qms, kss, vss, biasess = [], [], [], []
    for hp in range(NA_HEADS // 2):
        cs = slice(hp * LANES, (hp + 1) * LANES)
        q2 = q_ref[0, :, cs]
        zero = jnp.zeros_like(q2)
        qms += [jnp.where(lane < HEAD_DIM, q2, zero), jnp.where(lane >= HEAD_DIM, q2, zero)]
        kss += [[kl_ref[0, pl.ds(start, win), cs], kc_ref[0, :, cs]]] * 2
        vss += [[vl_ref[0, pl.ds(start, win), cs], vc_ref[0, :, cs]]] * 2
        biasess += [[bias_ref[0, 2 * hp], None], [bias_ref[0, 2 * hp + 1], None]]
    outs = _attend_phased(qms, kss, vss, biasess, s_ref, p_ref)
    for hp in range(NA_HEADS // 2):
        o_ref[0, :, hp * LANES:(hp + 1) * LANES] = jnp.where(row < HEAD_DIM, outs[2 * hp], outs[2 * hp + 1]).T.astype(BF)


def _na_bias_kind(k, n_steps):
    return jnp.where(k == 0, 0, jnp.where(k == n_steps - 1, 2, 1))


def _na_bias_table(rpb, rows):
    n_off = 2 * NA_KH - 1
    ck = np.arange(GRID_W)[:, None]
    cq = np.arange(GRID_W)[None, :]
    col_start = np.clip(cq - NA_KW // 2, 0, GRID_W - NA_KW)
    in_win = (ck >= col_start) & (ck < col_start + NA_KW)
    onehot = (ck - cq + (NA_KW - 1))[None] == np.arange(2 * NA_KW - 1)[:, None, None]
    tz = jnp.einsum('hrd,dkq->hrkq', rpb.astype(F32) * LOG2E, jnp.asarray(onehot, F32), precision=lax.Precision.HIGHEST)
    tz = jnp.where(jnp.asarray(in_win)[None, None], tz, NEG_BIG)
    tz = jnp.concatenate([tz, jnp.full((NA_HEADS, 1, GRID_W, GRID_W), NEG_BIG, F32)], axis=1)
    n_steps = rows // NA_QROWS
    idx = np.full((3, NA_QROWS, NA_WIN_ROWS), n_off, np.int32)
    for kind, k in enumerate((0, 1, n_steps - 1)):
        start_row = min(max(k * NA_QROWS - NA_KH // 2, 0), rows - NA_WIN_ROWS)
        for q in range(NA_QROWS):
            r = k * NA_QROWS + q
            r_start = min(max(r - NA_KH // 2, 0), rows - NA_KH)
            for w in range(NA_WIN_ROWS):
                a = start_row + w
                if r_start <= a < r_start + NA_KH:
                    idx[kind, q, w] = a - r + (NA_KH - 1)
    return pl.pallas_call(
        _na_bias_kernel,
        grid_spec=pltpu.PrefetchScalarGridSpec(
            num_scalar_prefetch=1,
            grid=(3, NA_HEADS),
            in_specs=[pl.BlockSpec((1, n_off + 1, GRID_W, GRID_W), lambda kind, h, idx_ref: (h, 0, 0, 0))],
            out_specs=pl.BlockSpec((1, 1, NA_WIN_ROWS * GRID_W, NA_QROWS * GRID_W),
                                   lambda kind, h, idx_ref: (kind, h, 0, 0)),
        ),
        out_shape=jax.ShapeDtypeStruct((3, NA_HEADS, NA_WIN_ROWS * GRID_W, NA_QROWS * GRID_W), F32),
        compiler_params=_cparams(("arbitrary", "arbitrary")),
        name="na_bias",
    )(jnp.asarray(idx.reshape(-1)), tz)


def _na_bias_kernel(idx_ref, tz_ref, o_ref):
    kind = pl.program_id(0)
    for w in range(NA_WIN_ROWS):
        for qp in range(NA_QROWS // 2):
            base = kind * (NA_QROWS * NA_WIN_ROWS) + w
            lo = tz_ref[0, idx_ref[base + (2 * qp) * NA_WIN_ROWS]]
            hi = tz_ref[0, idx_ref[base + (2 * qp + 1) * NA_WIN_ROWS]]
            o_ref[0, 0, w * GRID_W:(w + 1) * GRID_W, qp * LANES:(qp + 1) * LANES] = jnp.concatenate([lo, hi], axis=-1)


def _na_attention(p_lat, p_ctx, bias_tab, rows):
    b, s, _ = p_lat.shape
    lc = p_ctx.shape[1]
    base = (2 * A_QK + A_V) // B_QKV
    nq = NA_QROWS * GRID_W
    n_steps = rows // NA_QROWS
    return pl.pallas_call(
        functools.partial(_na_kernel, rows=rows),
        grid=(b, n_steps),
        in_specs=[
            pl.BlockSpec((1, nq, B_QKV), lambda i, k: (i, k, base)),
            pl.BlockSpec((1, s, B_QKV), lambda i, k: (i, 0, base + 1)),
            pl.BlockSpec((1, s, B_QKV), lambda i, k: (i, 0, base + 2)),
            pl.BlockSpec((1, lc, B_QKV), lambda i, k: (i, 0, base + 1)),
            pl.BlockSpec((1, lc, B_QKV), lambda i, k: (i, 0, base + 2)),
            pl.BlockSpec((1,) + bias_tab.shape[1:], lambda i, k: (_na_bias_kind(k, n_steps), 0, 0, 0)),
        ],
        out_specs=pl.BlockSpec((1, nq, B_QKV), lambda i, k: (i, k, 0)),
        out_shape=jax.ShapeDtypeStruct((b, s, B_QKV), BF),
        scratch_shapes=[pltpu.VMEM((NA_HEADS, NA_WIN_ROWS * GRID_W + lc, nq), F32),
                        pltpu.VMEM((NA_HEADS, NA_WIN_ROWS * GRID_W + lc, nq), BF)],
        compiler_params=_cparams(("arbitrary", "arbitrary")),
        name="na_attn",
    )(p_lat, p_lat, p_lat, p_ctx, p_ctx, bias_tab)


def _dense_ctx_kernel(q_ref, k_ref, v_ref, o_ref, s_ref, p_ref):
    lane = lax.broadcasted_iota(jnp.int32, (q_ref.shape[1], LANES), 1)
    row = lax.broadcasted_iota(jnp.int32, (LANES, q_ref.shape[1]), 0)
    qms, kss, vss = [], [], []
    for hp in range(NA_HEADS // 2):
        cs = slice(hp * LANES, (hp + 1) * LANES)
        q2 = q_ref[0, :, cs]
        zero = jnp.zeros_like(q2)
        qms += [jnp.where(lane < HEAD_DIM, q2, zero), jnp.where(lane >= HEAD_DIM, q2, zero)]
        kss += [[k_ref[0, :, cs]]] * 2
        vss += [[v_ref[0, :, cs]]] * 2
    outs = _attend_phased(qms, kss, vss, [[None]] * NA_HEADS, s_ref, p_ref)
    for hp in range(NA_HEADS // 2):
        o_ref[0, :, hp * LANES:(hp + 1) * LANES] = jnp.where(row < HEAD_DIM, outs[2 * hp], outs[2 * hp + 1]).T.astype(BF)


def _dense_ctx_attention(p_ctx):
    b, lc, _ = p_ctx.shape
    base = (2 * A_QK + A_V) // B_QKV
    spec = lambda j: pl.BlockSpec((1, lc, B_QKV), lambda i: (i, 0, base + j))
    return pl.pallas_call(
        _dense_ctx_kernel,
        grid=(b,),
        in_specs=[spec(0), spec(1), spec(2)],
        out_specs=pl.BlockSpec((1, lc, B_QKV), lambda i: (i, 0, 0)),
        out_shape=jax.ShapeDtypeStruct((b, lc, B_QKV), BF),
        scratch_shapes=[pltpu.VMEM((NA_HEADS, lc, lc), F32), pltpu.VMEM((NA_HEADS, lc, lc), BF)],
        compiler_params=_cparams(("arbitrary",)),
        name="dense_ctx_attn",
    )(p_ctx, p_ctx, p_ctx)


def _mlp_residual(x1, n2g, sc2, sh2, g2, w1_ref, w2_ref):
    h2 = _modulated_norm(x1, n2g, sc2, sh2).astype(BF)
    hc = 1024
    acc = jnp.zeros_like(x1)
    for c in range(MLP_HIDDEN // hc):
        hid = jnp.maximum(_dot(h2, w1_ref[:, c * hc:(c + 1) * hc]), 0.0)
        acc = acc + _dot((hid * hid).astype(BF), w2_ref[c * hc:(c + 1) * hc, :])
    return x1 + g2 * acc


def _even_tail_kernel(oa_ref, ob_ref, x_ref, wo_ref, w1_ref, w2_ref, n2g_ref, g1_ref, sc2_ref, sh2_ref, g2_ref,
                      n1g_ref, sc1n_ref, sh1n_ref, xo_ref, hn_ref):
    y = _dot(oa_ref[0], wo_ref[0:A_V, :]) + _dot(ob_ref[0], wo_ref[A_V:, :])
    x1 = x_ref[0] + g1_ref[0, 0] * y
    x2 = _mlp_residual(x1, n2g_ref[...], sc2_ref[0, 0], sh2_ref[0, 0], g2_ref[0, 0], w1_ref, w2_ref)
    xo_ref[0] = x2
    hn_ref[0] = _modulated_norm(x2, n1g_ref[...], sc1n_ref[0, 0], sh1n_ref[0, 0])


def _even_tail(oa, ob, x, wo, w1, w2, n2g, n1g, m4, layer, row, *, tm):
    b, s, d = x.shape
    tok = lambda w: pl.BlockSpec((1, tm, w), lambda i, t: (i, t, 0))
    mod = lambda lyr, col: _mod_spec(d, lyr, col, row)
    vec = pl.BlockSpec((1, d), lambda i, t: (0, 0))
    full = lambda a: pl.BlockSpec(a.shape, lambda i, t: (0, 0))
    return pl.pallas_call(
        _even_tail_kernel,
        grid=(b, s // tm),
        in_specs=[tok(A_V), tok(B_QKV), tok(d), full(wo), full(w1), full(w2), vec,
                  mod(layer, MOD_G1), mod(layer, MOD_SC2), mod(layer, MOD_SH2), mod(layer, MOD_G2), vec,
                  mod(layer + 1, MOD_SC1), mod(layer + 1, MOD_SH1)],
        out_specs=[tok(d), tok(d)],
        out_shape=[jax.ShapeDtypeStruct((b, s, d), F32), jax.ShapeDtypeStruct((b, s, d), F32)],
        compiler_params=_cparams(("arbitrary", "arbitrary")),
        name="even_tail",
    )(oa, ob, x, wo, w1, w2, n2g, m4, m4, m4, m4, n1g, m4, m4)


def _odd_tail_kernel(y_ref, h_ref, x_ref, dsk_ref, wa_ref, wb_ref, w1_ref, w2_ref, n2g_ref, g1_ref, sc2_ref, sh2_ref,
                     g2_ref, fg_ref, o_ref):
    z = y_ref[0] + dsk_ref[...] * h_ref[0]
    z = jax.nn.gelu(z).astype(BF)
    mix = _dot(z, wa_ref[...]) * jax.nn.sigmoid(_dot(z, wb_ref[...]))
    x1 = x_ref[0] + g1_ref[0, 0] * mix
    x2 = _mlp_residual(x1, n2g_ref[...], sc2_ref[0, 0], sh2_ref[0, 0], g2_ref[0, 0], w1_ref, w2_ref)
    o_ref[0] = _rms(x2, NORM_EPS) * fg_ref[...]


def _odd_tail(y, h, x, dsk, wa, wb, w1, w2, n2g, fg, m4, layer, *, tm):
    b, s, d = x.shape
    tok = pl.BlockSpec((1, tm, d), lambda i, t: (i, t, 0))
    mod = lambda col: _mod_spec(d, layer, col, None)
    vec = pl.BlockSpec((1, d), lambda i, t: (0, 0))
    full = lambda a: pl.BlockSpec(a.shape, lambda i, t: (0, 0))
    return pl.pallas_call(
        _odd_tail_kernel,
        grid=(b, s // tm),
        in_specs=[tok, tok, tok, vec, full(wa), full(wb), full(w1), full(w2), vec,
                  mod(MOD_G1), mod(MOD_SC2), mod(MOD_SH2), mod(MOD_G2), vec],
        out_specs=tok,
        out_shape=jax.ShapeDtypeStruct((b, s, d), F32),
        compiler_params=_cparams(("arbitrary", "arbitrary")),
        name="odd_tail",
    )(y, h, x, dsk, wa, wb, w1, w2, n2g, m4, m4, m4, m4, fg)


def _s5_ops_kernel(prm_ref, bt_ref, c_ref, toep_ref, win_ref, wst_ref, at_ref):
    t_len = SSM_CHUNK
    p_dim = SSM_STATE
    pair = pl.program_id(0)
    pos = lax.broadcasted_iota(jnp.int32, (t_len, 1), 0)
    row_pos = lax.broadcasted_iota(jnp.int32, (SSM_CW, 1), 0) // SSM_GROUP
    col_pos = lax.broadcasted_iota(jnp.int32, (1, SSM_CW), 1) // SSM_GROUP
    wst_ref[...] = jnp.zeros(wst_ref.shape, wst_ref.dtype)
    zeros_half = jnp.zeros((SSM_CW, p_dim), F32)
    a_rows = [[], [], [], []]
    for jj in range(2):
        m = (2 * pair + jj) % SUBLANES
        m_out = (SUBLANES - m) % SUBLANES

        def tok_in(p):
            return (p // SUBLANES) * SUBLANES + (p % SUBLANES + m) % SUBLANES

        def tok_out(p):
            return (p // SUBLANES) * SUBLANES + (m_out + SUBLANES - p % SUBLANES) % SUBLANES

        e_in = tok_in(pos).astype(F32)
        e_out = tok_out(pos).astype(F32)
        toep = None
        win_units = []
        for d in range(2):
            lr = prm_ref[d, jj, 0:1, :]
            li = prm_ref[d, jj, 1:2, :]
            delta = jnp.exp(prm_ref[d, jj, 2:3, :])
            dec = lr * delta
            ang = li * delta

            def apow(e):
                mag = jnp.exp(dec * e)
                return mag * jnp.cos(ang * e), mag * jnp.sin(ang * e)

            ar1, ai1 = apow(1.0)
            den = lr * lr + li * li
            nr = ar1 - 1.0
            cr = (nr * lr + ai1 * li) / den
            ci = (ai1 * lr - nr * li) / den
            bbr = cr * bt_ref[d, jj, 0] - ci * bt_ref[d, jj, 1]
            bbi = cr * bt_ref[d, jj, 1] + ci * bt_ref[d, jj, 0]
            cre = c_ref[d, jj, 0]
            cim = c_ref[d, jj, 1]

            def factor(e, xr, xi):
                ar, ai = apow(e)
                re = ar[:, None, :] * xr[None] - ai[:, None, :] * xi[None]
                im = ar[:, None, :] * xi[None] + ai[:, None, :] * xr[None]
                return re.reshape(SSM_CW, p_dim), im.reshape(SSM_CW, p_dim)

            sgn = 1.0 if d == 0 else -1.0
            lre, lim = factor(-sgn * e_in, bbr, bbi)
            rre, rim = factor(sgn * e_out, cre, cim)
            full = lax.dot_general(jnp.concatenate([lre, lim], axis=-1), jnp.concatenate([rre, -rim], axis=-1),
                                   (((1,), (1,)), ((), ())), precision=lax.Precision.HIGHEST,
                                   preferred_element_type=F32)
            if d == 0:
                valid = tok_out(col_pos) >= tok_in(row_pos)
            else:
                valid = tok_in(row_pos) >= tok_out(col_pos)
            part = jnp.where(valid, full, 0.0)
            toep = part if toep is None else toep + part

            wre, wim = factor((t_len - 1) - e_in, bbr, bbi) if d == 0 else (lre, lim)
            for x in (wre, wim):
                win_units.append(jnp.concatenate([x, zeros_half] if jj == 0 else [zeros_half, x], axis=-1))
            sre, sim = factor(e_out + 1.0 if d == 0 else t_len - e_out, cre, cim)
            st = jnp.concatenate([sre, -sim], axis=-1).T
            for comp in range(2):
                r0 = (2 * d + comp) * 2 * p_dim + jj * p_dim
                wst_ref[jj, r0:r0 + p_dim, :] = st[comp * p_dim:(comp + 1) * p_dim].astype(BF)
            atr, ati = apow(float(t_len))
            a_rows[2 * d].append(atr)
            a_rows[2 * d + 1].append(ati)
        toep_ref[jj] = toep.astype(BF)
        win_ref[jj] = jnp.concatenate(win_units, axis=-1).astype(BF)
    at_ref[0] = jnp.concatenate([jnp.concatenate(r, axis=-1) for r in a_rows], axis=0)


def _s5_operators(lam_re, lam_im, log_step, b_re, b_im, c_re, c_im):
    q = np.arange(SSM_GROUPS)
    group = (q // SUBLANES) * SUBLANES + (SUBLANES - q % SUBLANES) % SUBLANES
    g = SSM_GROUPS
    p_dim = SSM_STATE
    prm = jnp.stack([lam_re.astype(F32), lam_im.astype(F32),
                     jnp.broadcast_to(log_step.astype(F32)[..., None], lam_re.shape)], axis=2)[:, group]
    bt = jnp.stack([b_re.astype(F32), b_im.astype(F32)], axis=2).transpose(0, 1, 2, 4, 3)[:, group]
    cc = jnp.stack([c_re.astype(F32), c_im.astype(F32)], axis=2)[:, group]
    cw = SSM_CW
    return pl.pallas_call(
        _s5_ops_kernel,
        grid=(g // 2,),
        in_specs=[pl.BlockSpec((2, 2, 3, p_dim), lambda i: (0, i, 0, 0)),
                  pl.BlockSpec((2, 2, 2, SSM_GROUP, p_dim), lambda i: (0, i, 0, 0, 0)),
                  pl.BlockSpec((2, 2, 2, SSM_GROUP, p_dim), lambda i: (0, i, 0, 0, 0))],
        out_specs=[pl.BlockSpec((2, cw, cw), lambda i: (i, 0, 0)),
                   pl.BlockSpec((2, cw, 8 * p_dim), lambda i: (i, 0, 0)),
                   pl.BlockSpec((2, 8 * p_dim, cw), lambda i: (i, 0, 0)),
                   pl.BlockSpec((1, 4, 2 * p_dim), lambda i: (i, 0, 0))],
        out_shape=[jax.ShapeDtypeStruct((g, cw, cw), BF), jax.ShapeDtypeStruct((g, cw, 8 * p_dim), BF),
                   jax.ShapeDtypeStruct((g, 8 * p_dim, cw), BF), jax.ShapeDtypeStruct((g // 2, 4, 2 * p_dim), F32)],
        compiler_params=_cparams(("arbitrary",)),
        name="s5_ops",
    )(prm, bt, cc)


def _to_slots(x):
    a = pltpu.roll(x, 0, 1, stride=SSM_GROUP, stride_axis=0)
    a = a.reshape(x.shape[0] // SUBLANES, SUBLANES, LANES)
    blk = lax.broadcasted_iota(jnp.int32, a.shape, 2) // SSM_GROUP
    for bit in (1, 2, 4):
        a = jnp.where((blk & bit) != 0, pltpu.roll(a, SUBLANES - bit, 1), a)
    return a


def _from_slots(t):
    blk = lax.broadcasted_iota(jnp.int32, t.shape, 2) // SSM_GROUP
    for bit in (1, 2, 4):
        t = jnp.where((blk & bit) != 0, pltpu.roll(t, SUBLANES - bit, 1), t)
    a = t.reshape(t.shape[0] * SUBLANES, LANES)
    return pltpu.roll(a, 0, 1, stride=SSM_GROUP, stride_axis=0)


def _s5_pack_kernel(h_ref, hc_ref, o_ref, *, n_ctx, n_lat):
    def body(b, carry):
        tc = _to_slots(hc_ref[b]).reshape(n_ctx, 2, SUBLANES, LANES)
        tl = _to_slots(h_ref[b]).reshape(n_lat, 2, SUBLANES, LANES)
        for o in range(2):
            o_ref[0, o, 0:n_ctx, b] = tc[:, o]
            o_ref[0, o, n_ctx:, b] = tl[:, o]
        return carry

    lax.fori_loop(0, h_ref.shape[0], body, 0)


def _s5_pack(h, hc):
    b, s, d = h.shape
    lc = hc.shape[1]
    n_ctx, n_lat = lc // SSM_CHUNK, s // SSM_CHUNK
    shape = (d // LANES, 2, n_ctx + n_lat, b, SUBLANES, LANES)
    return pl.pallas_call(
        functools.partial(_s5_pack_kernel, n_ctx=n_ctx, n_lat=n_lat),
        grid=(d // LANES,),
        in_specs=[pl.BlockSpec((b, s, LANES), lambda j: (0, 0, j)), pl.BlockSpec((b, lc, LANES), lambda j: (0, 0, j))],
        out_specs=pl.BlockSpec((1,) + shape[1:], lambda j: (j, 0, 0, 0, 0, 0)),
        out_shape=jax.ShapeDtypeStruct(shape, F32),
        compiler_params=_cparams(("arbitrary",)),
        name="s5_pack",
    )(h, hc)


def _s5_unpack_kernel(y_ref, o_ref, *, n_lat):
    def body(b, carry):
        t = jnp.stack([y_ref[0, 0, :, b], y_ref[0, 1, :, b]], axis=1).reshape(n_lat * 2, SUBLANES, LANES)
        o_ref[b] = _from_slots(t)
        return carry

    lax.fori_loop(0, o_ref.shape[0], body, 0)


def _s5_unpack(y, b):
    nj, _, n_lat = y.shape[:3]
    s = n_lat * SSM_CHUNK
    return pl.pallas_call(
        functools.partial(_s5_unpack_kernel, n_lat=n_lat),
        grid=(nj,),
        in_specs=[pl.BlockSpec((1,) + y.shape[1:], lambda j: (j, 0, 0, 0, 0, 0))],
        out_specs=pl.BlockSpec((b, s, LANES), lambda j: (0, 0, j)),
        out_shape=jax.ShapeDtypeStruct((b, s, nj * LANES), F32),
        compiler_params=_cparams(("arbitrary",)),
        name="s5_unpack",
    )(y)


def _s5_kernel(u_ref, toep_ref, win_ref, wst_ref, at_ref, y_ref, sloc_ref, sin_ref, *, n_ctx, n_lat, nb):
    p = pl.program_id(1)
    w = 2 * SSM_STATE
    n_rows = (n_ctx + n_lat) * nb
    lat0 = n_ctx * nb
    slots = [2 * p, 2 * p + 1]
    us = [jnp.concatenate([u_ref[0, o, pl.ds(m, n_rows, stride=SUBLANES), :] for o in range(2)], axis=-1).astype(BF)
          for m in slots]
    sloc_ref[...] = _dot(us[0], win_ref[0]) + _dot(us[1], win_ref[1])

    afr = jnp.broadcast_to(at_ref[0, 0:1, :], (nb, w))
    afi = jnp.broadcast_to(at_ref[0, 1:2, :], (nb, w))
    abr = jnp.broadcast_to(at_ref[0, 2:3, :], (nb, w))
    abi = jnp.broadcast_to(at_ref[0, 3:4, :], (nb, w))

    def advance(carry, rf, rb):
        fr, fi, br, bi = carry
        xfr = sloc_ref[pl.ds(rf, nb), 0:w]
        xfi = sloc_ref[pl.ds(rf, nb), w:2 * w]
        xbr = sloc_ref[pl.ds(rb, nb), 2 * w:3 * w]
        xbi = sloc_ref[pl.ds(rb, nb), 3 * w:4 * w]
        return (afr * fr - afi * fi + xfr, afr * fi + afi * fr + xfi,
                abr * br - abi * bi + xbr, abr * bi + abi * br + xbi)

    def ctx_body(i, carry):
        rf = pl.multiple_of(i * nb, nb)
        rb = pl.multiple_of((n_ctx - 1 - i) * nb, nb)
        return advance(carry, rf, rb)

    def lat_body(i, carry):
        fr, fi, br, bi = carry
        rf = pl.multiple_of(i * nb, nb)
        rb = pl.multiple_of((n_lat - 1 - i) * nb, nb)
        sin_ref[pl.ds(rf, nb), 0:w] = fr
        sin_ref[pl.ds(rf, nb), w:2 * w] = fi
        sin_ref[pl.ds(rb, nb), 2 * w:3 * w] = br
        sin_ref[pl.ds(rb, nb), 3 * w:4 * w] = bi
        return advance(carry, lat0 + rf, lat0 + rb)

    zero = jnp.zeros((nb, w), F32)
    carry = lax.fori_loop(0, n_ctx, ctx_body, (zero, zero, zero, zero))
    lax.fori_loop(0, n_lat, lat_body, carry)

    s_in = sin_ref[...].astype(BF)
    for q in range(2):
        y = _dot(us[q][lat0:], toep_ref[q]) + _dot(s_in, wst_ref[q])
        m_out = (SUBLANES - slots[q]) % SUBLANES
        for o in range(2):
            y_ref[0, o, pl.ds(m_out, n_lat * nb, stride=SUBLANES), :] = y[:, o * LANES:(o + 1) * LANES]


def _s5_scan(u, toep, w_in, w_st, a_t, *, n_ctx, nb):
    nj, _, r8, _ = u.shape
    n_lat = r8 // (nb * SUBLANES) - n_ctx
    cw = SSM_CW
    n_pairs = SUBLANES // 2
    pair = lambda r, c: pl.BlockSpec((2, r, c), lambda j, p: (j * n_pairs + p, 0, 0))
    return pl.pallas_call(
        functools.partial(_s5_kernel, n_ctx=n_ctx, n_lat=n_lat, nb=nb),
        grid=(nj, n_pairs),
        in_specs=[pl.BlockSpec((1, 2, r8, LANES), lambda j, p: (j, 0, 0, 0)),
                  pair(cw, cw), pair(cw, 8 * SSM_STATE), pair(8 * SSM_STATE, cw),
                  pl.BlockSpec((1, 4, 2 * SSM_STATE), lambda j, p: (j * n_pairs + p, 0, 0))],
        out_specs=pl.BlockSpec((1, 2, n_lat * nb * SUBLANES, LANES), lambda j, p: (j, 0, 0, 0)),
        out_shape=jax.ShapeDtypeStruct((nj, 2, n_lat * nb * SUBLANES, LANES), F32),
        scratch_shapes=[pltpu.VMEM(((n_ctx + n_lat) * nb, 8 * SSM_STATE), F32),
                        pltpu.VMEM((n_lat * nb, 8 * SSM_STATE), F32)],
        compiler_params=_cparams(("arbitrary", "arbitrary")),
        name="s5_scan",
    )(u, toep, w_in, w_st, a_t)


def _s5_mix(h, hc, params):
    b, s, d = h.shape
    n_ctx = hc.shape[1] // SSM_CHUNK
    n_lat = s // SSM_CHUNK
    toep, w_si, w_st, a_t = _s5_operators(*params)
    u = _s5_pack(h, hc).reshape(d // LANES, 2, (n_ctx + n_lat) * b * SUBLANES, LANES)
    y = _s5_scan(u, toep, w_si, w_st, a_t, n_ctx=n_ctx, nb=b)
    return _s5_unpack(y.reshape(d // LANES, 2, n_lat, b, SUBLANES, LANES), b)


def kernel(x, c, ctx, c_ctx, w_ada, b_ada, norm1_g, norm2_g, final_g, w_in, w_out, lam_q1, lam_k1, lam_q2, lam_k2,
           subln_g, na_rpb, ssm_lam_re, ssm_lam_im, ssm_log_step, ssm_b_re, ssm_b_im, ssm_c_re, ssm_c_im, ssm_d,
           glu_w_a, glu_w_b, mlp_w1, mlp_w2):
    b, s, d = x.shape
    lc = ctx.shape[1]
    rows = s // GRID_W
    assert DEPTH == 2 and d == D_MODEL and s % GRID_W == 0 and rows >= NA_WIN_ROWS and rows % NA_QROWS == 0
    assert s % SSM_CHUNK == 0 and lc % SSM_CHUNK == 0 and b % SUBLANES == 0

    n_act = -(-(b + 1) // SUBLANES) * SUBLANES
    act_in = jnp.zeros((n_act, d), F32).at[:b].set(c).at[b].set(c_ctx)
    m4 = _ada(act_in, w_ada, b_ada).reshape(DEPTH, n_act, 1, 6 * d)
    n1g = [norm1_g[i].reshape(1, d) for i in range(DEPTH)]
    n2g = [norm2_g[i].reshape(1, d) for i in range(DEPTH)]

    cos_t, sin_t = _rope_tables(s)
    w_in_b = w_in[0].astype(BF)
    p_lat = _inproj(x, n1g[0], m4, 0, None, w_in_b, cos_t, sin_t, rope=True, tm=1024)
    p_ctx = _inproj(ctx, n1g[0], m4, 0, b, w_in_b, cos_t[:lc], sin_t[:lc], rope=False, tm=lc)

    lam_init = 0.8 - 0.6 * math.exp(-0.3 * 0)
    lam_p = jnp.stack([lam_q1[0], lam_k1[0], lam_q2[0], lam_k2[0]]).astype(F32)
    gsub = subln_g[0].reshape(1, 2 * HEAD_DIM).astype(F32)
    oa = _diff_attention(lam_p, gsub, p_lat, [p_lat, p_ctx], tq=512, lam_init=lam_init, name="diff_attn")
    oac = _diff_attention(lam_p, gsub, p_ctx, [p_ctx], tq=lc, lam_init=lam_init, name="diff_attn_ctx")
    ob = _na_attention(p_lat, p_ctx, _na_bias_table(na_rpb[0], rows), rows)
    obc = _dense_ctx_attention(p_ctx)

    wo_b = w_out[0].astype(BF)
    w1_b = [mlp_w1[i].astype(BF) for i in range(DEPTH)]
    w2_b = [mlp_w2[i].astype(BF) for i in range(DEPTH)]
    x1, h1 = _even_tail(oa, ob, x, wo_b, w1_b[0], w2_b[0], n2g[0], n1g[1], m4, 0, None, tm=512)
    _, hc1 = _even_tail(oac, obc, ctx, wo_b, w1_b[0], w2_b[0], n2g[0], n1g[1], m4, 0, b, tm=lc)

    y = _s5_mix(h1, hc1, (ssm_lam_re[0], ssm_lam_im[0], ssm_log_step[0], ssm_b_re[0], ssm_b_im[0], ssm_c_re[0],
                          ssm_c_im[0]))
    return _odd_tail(y, h1, x1, ssm_d[0].reshape(1, d).astype(F32), glu_w_a[0].astype(BF), glu_w_b[0].astype(BF),
                     w1_b[1], w2_b[1], n2g[1], final_g.reshape(1, d).astype(F32), m4, 1, tm=512)
```

```python
import functools
import math

import jax
import jax.numpy as jnp
import numpy as np
from jax import lax
from jax.experimental import pallas as pl
from jax.experimental.pallas import tpu as pltpu

D_MODEL = 1024
DEPTH = 2
GRID_W = 64
HEAD_DIM = 64
DIFF_HEADS = 4
NA_HEADS = 8
NA_KH = 8
NA_KW = 16
NA_QROWS = 4
NA_WIN_ROWS = 12
ATTN_Q_SUB = 256
ROPE_BASE = 10000.0
ROPE_AXIS_DIM = HEAD_DIM // 2
SSM_GROUP = 16
SSM_GROUPS = D_MODEL // SSM_GROUP
SSM_STATE = 64
MLP_HIDDEN = 4 * D_MODEL
A_QK = DIFF_HEADS * 2 * HEAD_DIM
A_V = DIFF_HEADS * 2 * HEAD_DIM
B_QKV = NA_HEADS * HEAD_DIM
IN_PROJ = 2 * A_QK + A_V + 3 * B_QKV
NORM_EPS = 1e-6
SUBLN_EPS = 1e-5

LANES = 128
SUBLANES = 8
SSM_CHUNK = 16
SSM_CW = SSM_CHUNK * SSM_GROUP
NEG_BIG = -1e30
LOG2E = math.log2(math.e)
VMEM_LIMIT = 56 * 1024 * 1024

BF = jnp.bfloat16
F32 = jnp.float32


def _cparams(sem):
    return pltpu.CompilerParams(dimension_semantics=sem, vmem_limit_bytes=VMEM_LIMIT)


def _dot(a, b):
    return jnp.dot(a, b, preferred_element_type=F32)


def _dot_nt(a, b):
    return lax.dot_general(a, b, (((1,), (1,)), ((), ())), preferred_element_type=F32)


def _dot_tn(a, b):
    return lax.dot_general(a, b, (((0,), (0,)), ((), ())), preferred_element_type=F32)


def _rms(x, eps):
    return x * lax.rsqrt(jnp.mean(x * x, axis=-1, keepdims=True) + eps)


def _ada_kernel(a_ref, w_ref, b_ref, o_ref):
    a = a_ref[...]
    act = a * jax.nn.sigmoid(a)
    o_ref[0] = _dot(act.astype(BF), w_ref[0].astype(BF)) + b_ref[0]


def _ada(act_in, w_ada, b_ada):
    r, d = act_in.shape
    n = w_ada.shape[-1]
    tn = 1536
    return pl.pallas_call(
        _ada_kernel,
        grid=(DEPTH, n // tn),
        in_specs=[
            pl.BlockSpec((r, d), lambda i, j: (0, 0)),
            pl.BlockSpec((1, d, tn), lambda i, j: (i, 0, j)),
            pl.BlockSpec((1, 1, tn), lambda i, j: (i, 0, j)),
        ],
        out_specs=pl.BlockSpec((1, r, tn), lambda i, j: (i, 0, j)),
        out_shape=jax.ShapeDtypeStruct((DEPTH, r, n), F32),
        compiler_params=_cparams(("arbitrary", "arbitrary")),
        name="ada_ln",
    )(act_in, w_ada, b_ada.reshape(DEPTH, 1, n))


def _modulated_norm(x, g, sc, sh):
    return (_rms(x, NORM_EPS) * g) * (1.0 + sc) + sh


def _rope_chunk(x, cos, sin, low):
    up = pltpu.roll(x, LANES - 16, 1)
    dn = pltpu.roll(x, 16, 1)
    return x * cos + jnp.where(low, up, dn) * sin


def _inproj_kernel(x_ref, g_ref, sc_ref, sh_ref, w_ref, cos_ref, sin_ref, o_ref, *, rope):
    h = _modulated_norm(x_ref[0], g_ref[...], sc_ref[0, 0], sh_ref[0, 0]).astype(BF)
    tm = h.shape[0]
    if rope:
        cos = cos_ref[...]
        sin = sin_ref[...]
        low = (lax.broadcasted_iota(jnp.int32, (tm, LANES), 1) % 32) < 16
    cw = 512
    for j in range(IN_PROJ // cw):
        acc = _dot(h, w_ref[:, j * cw:(j + 1) * cw])
        if j in (0, 3):
            acc = acc * (HEAD_DIM ** -0.5 * LOG2E)
        if rope and j < 2:
            parts = [_rope_chunk(acc[:, c * LANES:(c + 1) * LANES], cos, sin, low) for c in range(cw // LANES)]
            acc = jnp.concatenate(parts, axis=-1)
        o_ref[0, :, j * cw:(j + 1) * cw] = acc.astype(BF)


def _mod_spec(d, layer, col, row):
    if row is None:
        return pl.BlockSpec((1, 1, 1, d), lambda i, t: (layer, i, 0, col))
    return pl.BlockSpec((1, 1, 1, d), lambda i, t: (layer, row, 0, col))


MOD_SH1, MOD_SC1, MOD_G1, MOD_SH2, MOD_SC2, MOD_G2 = range(6)


def _inproj(x, g, m4, layer, row, w, cos_t, sin_t, *, rope, tm):
    b, s, d = x.shape
    n = w.shape[-1]
    return pl.pallas_call(
        functools.partial(_inproj_kernel, rope=rope),
        grid=(b, s // tm),
        in_specs=[
            pl.BlockSpec((1, tm, d), lambda i, t: (i, t, 0)),
            pl.BlockSpec((1, d), lambda i, t: (0, 0)),
            _mod_spec(d, layer, MOD_SC1, row), _mod_spec(d, layer, MOD_SH1, row),
            pl.BlockSpec((d, n), lambda i, t: (0, 0)),
            pl.BlockSpec((tm, LANES), lambda i, t: (t, 0)),
            pl.BlockSpec((tm, LANES), lambda i, t: (t, 0)),
        ],
        out_specs=pl.BlockSpec((1, tm, n), lambda i, t: (i, t, 0)),
        out_shape=jax.ShapeDtypeStruct((b, s, n), BF),
        compiler_params=_cparams(("arbitrary", "arbitrary")),
        name="in_proj_rope" if rope else "in_proj_ctx",
    )(x, g, m4, m4, w, cos_t, sin_t)


def _rope_tables(seq):
    t = jnp.arange(seq)
    row = (t // GRID_W).astype(F32)
    col = (t % GRID_W).astype(F32)
    n_freq = ROPE_AXIS_DIM // 2
    inv_freq = ROPE_BASE ** (-jnp.arange(n_freq, dtype=F32) / n_freq)
    ang_r = row[:, None] * inv_freq[None, :]
    ang_c = col[:, None] * inv_freq[None, :]
    cos64 = jnp.concatenate([jnp.cos(ang_r), jnp.cos(ang_r), jnp.cos(ang_c), jnp.cos(ang_c)], axis=-1)
    sin64 = jnp.concatenate([-jnp.sin(ang_r), jnp.sin(ang_r), -jnp.sin(ang_c), jnp.sin(ang_c)], axis=-1)
    return jnp.tile(cos64, (1, 2)), jnp.tile(sin64, (1, 2))


def _diff_lambda(lam_ref, lam_init):
    lp = lam_ref[...]
    s1 = jnp.sum(lp[0:1] * lp[1:2], axis=-1, keepdims=True)
    s2 = jnp.sum(lp[2:3] * lp[3:4], axis=-1, keepdims=True)
    return jnp.exp(s1) - jnp.exp(s2) + lam_init


def _attend_phased(qms, kss, vss, biasess, s_ref, p_ref):
    n_chain = len(qms)
    offs = [np.cumsum([0] + [k.shape[0] for k in ks]) for ks in kss]
    maxes = []
    for c in range(n_chain):
        parts = []
        for i, k in enumerate(kss[c]):
            s = _dot_nt(k, qms[c])
            if biasess[c][i] is not None:
                s = s + biasess[c][i]
            s_ref[c, offs[c][i]:offs[c][i + 1], :] = s
            parts.append(jnp.max(s, axis=0, keepdims=True))
        maxes.append(functools.reduce(jnp.maximum, parts))
    dens = []
    for c in range(n_chain):
        p = jnp.exp2(s_ref[c] - maxes[c])
        dens.append(jnp.sum(p, axis=0, keepdims=True))
        p_ref[c] = p.astype(BF)
    outs = []
    for c in range(n_chain):
        nums = [_dot_tn(v, p_ref[c, offs[c][i]:offs[c][i + 1], :]) for i, v in enumerate(vss[c])]
        outs.append(functools.reduce(jnp.add, nums) / dens[c])
    return outs


def _diff_head(q, ks, vs, lam, gsub, lam_init, s_ref, p_ref):
    lane = lax.broadcasted_iota(jnp.int32, q.shape, 1)
    zero = jnp.zeros_like(q)
    n_sub = q.shape[0] // s_ref.shape[2]
    q_maps = [jnp.where(lane < HEAD_DIM, q, zero), jnp.where(lane >= HEAD_DIM, q, zero)]
    qms = [qm[j * s_ref.shape[2]:(j + 1) * s_ref.shape[2]] for j in range(n_sub) for qm in q_maps]
    n_chain = len(qms)
    outs = _attend_phased(qms, [ks] * n_chain, [vs] * n_chain, [[None] * len(ks)] * n_chain, s_ref, p_ref)
    o = jnp.concatenate([(outs[2 * j] - lam * outs[2 * j + 1]).T for j in range(n_sub)], axis=0)
    return (_rms(o, SUBLN_EPS) * gsub) * (1.0 - lam_init)


def _diff_kernel(lam_ref, gs_ref, q_ref, *refs, n_kv, lam_init):
    k_refs, v_refs = refs[:n_kv], refs[n_kv:2 * n_kv]
    o_ref, s_ref, p_ref = refs[2 * n_kv:]
    lam = _diff_lambda(lam_ref, lam_init)
    o = _diff_head(q_ref[0], [r[0] for r in k_refs], [r[0] for r in v_refs], lam, gs_ref[...], lam_init,
                   s_ref, p_ref)
    o_ref[0] = o.astype(BF)


def _diff_attention(lam_p, gsub, q_src, kv_srcs, *, tq, lam_init, name):
    b, sq, _ = q_src.shape
    hw = 2 * HEAD_DIM
    k0, v0 = A_QK // hw, 2 * A_QK // hw
    n_keys = sum(s.shape[1] for s in kv_srcs)
    sub = min(tq, ATTN_Q_SUB)
    specs = [
        pl.BlockSpec(lam_p.shape, lambda i, h, t: (0, 0)),
        pl.BlockSpec((1, hw), lambda i, h, t: (0, 0)),
        pl.BlockSpec((1, tq, hw), lambda i, h, t: (i, t, h)),
    ]
    specs += [pl.BlockSpec((1, s.shape[1], hw), lambda i, h, t: (i, 0, k0 + h)) for s in kv_srcs]
    specs += [pl.BlockSpec((1, s.shape[1], hw), lambda i, h, t: (i, 0, v0 + h)) for s in kv_srcs]
    return pl.pallas_call(
        functools.partial(_diff_kernel, n_kv=len(kv_srcs), lam_init=lam_init),
        grid=(b, DIFF_HEADS, sq // tq),
        in_specs=specs,
        out_specs=pl.BlockSpec((1, tq, hw), lambda i, h, t: (i, t, h)),
        out_shape=jax.ShapeDtypeStruct((b, sq, A_V), BF),
        scratch_shapes=[pltpu.VMEM((2 * tq // sub, n_keys, sub), F32), pltpu.VMEM((2 * tq // sub, n_keys, sub), BF)],
        compiler_params=_cparams(("arbitrary", "arbitrary", "arbitrary")),
        name=name,
    )(lam_p, gsub, q_src, *kv_srcs, *kv_srcs)


def _na_kernel(q_ref, kl_ref, vl_ref, kc_ref, vc_ref, bias_ref, o_ref, s_ref, p_ref, *, rows):
    k = pl.program_id(1)
    start_row = jnp.clip(k * NA_QROWS - NA_KH // 2, 0, rows - NA_WIN_ROWS)
    nq = NA_QROWS * GRID_W
    win = NA_WIN_ROWS * GRID_W
    start = pl.multiple_of(start_row * GRID_W, nq)
    lane = lax.broadcasted_iota(jnp.int32, (nq, LANES), 1)
    row = lax.broadcasted_iota(jnp.int32, (LANES, nq), 0)
    qms, kss, vss, biasess = [], [], [], []
    for hp in range(NA_HEADS // 2):
        cs = slice(hp * LANES, (hp + 1) * LANES)
        q2 = q_ref[0, :, cs]
        zero = jnp.zeros_like(q2)
        qms += [jnp.where(lane < HEAD_DIM, q2, zero), jnp.where(lane >= HEAD_DIM, q2, zero)]
        kss += [[kl_ref[0, pl.ds(start, win), cs], kc_ref[0, :, cs]]] * 2
        vss += [[vl_ref[0, pl.ds(start, win), cs], vc_ref[0, :, cs]]] * 2
        biasess += [[bias_ref[0, 2 * hp], None], [bias_ref[0, 2 * hp + 1], None]]
    outs = _attend_phased(qms, kss, vss, biasess, s_ref, p_ref)
    for hp in range(NA_HEADS // 2):
        o_ref[0, :, hp * LANES:(hp + 1) * LANES] = jnp.where(row < HEAD_DIM, outs[2 * hp], outs[2 * hp + 1]).T.astype(BF)


def _na_bias_kind(k, n_steps):
    return jnp.where(k == 0, 0, jnp.where(k == n_steps - 1, 2, 1))


def _na_bias_table(rpb, rows):
    n_off = 2 * NA_KH - 1
    ck = np.arange(GRID_W)[:, None]
    cq = np.arange(GRID_W)[None, :]
    col_start = np.clip(cq - NA_KW // 2, 0, GRID_W - NA_KW)
    in_win = (ck >= col_start) & (ck < col_start + NA_KW)
    onehot = (ck - cq + (NA_KW - 1))[None] == np.arange(2 * NA_KW - 1)[:, None, None]
    tz = jnp.einsum('hrd,dkq->hrkq', rpb.astype(F32) * LOG2E, jnp.asarray(onehot, F32), precision=lax.Precision.HIGHEST)
    tz = jnp.where(jnp.asarray(in_win)[None, None], tz, NEG_BIG)
    tz = jnp.concatenate([tz, jnp.full((NA_HEADS, 1, GRID_W, GRID_W), NEG_BIG, F32)], axis=1)
    n_steps = rows // NA_QROWS
    idx = np.full((3, NA_QROWS, NA_WIN_ROWS), n_off, np.int32)
    for kind, k in enumerate((0, 1, n_steps - 1)):
        start_row = min(max(k * NA_QROWS - NA_KH // 2, 0), rows - NA_WIN_ROWS)
        for q in range(NA_QROWS):
            r = k * NA_QROWS + q
            r_start = min(max(r - NA_KH // 2, 0), rows - NA_KH)
            for w in range(NA_WIN_ROWS):
                a = start_row + w
                if r_start <= a < r_start + NA_KH:
                    idx[kind, q, w] = a - r + (NA_KH - 1)
    return pl.pallas_call(
        _na_bias_kernel,
        grid_spec=pltpu.PrefetchScalarGridSpec(
            num_scalar_prefetch=1,
            grid=(3, NA_HEADS),
            in_specs=[pl.BlockSpec((1, n_off + 1, GRID_W, GRID_W), lambda kind, h, idx_ref: (h, 0, 0, 0))],
            out_specs=pl.BlockSpec((1, 1, NA_WIN_ROWS * GRID_W, NA_QROWS * GRID_W),
                                   lambda kind, h, idx_ref: (kind, h, 0, 0)),
        ),
        out_shape=jax.ShapeDtypeStruct((3, NA_HEADS, NA_WIN_ROWS * GRID_W, NA_QROWS * GRID_W), F32),
        compiler_params=_cparams(("arbitrary", "arbitrary")),
        name="na_bias",
    )(jnp.asarray(idx.reshape(-1)), tz)


def _na_bias_kernel(idx_ref, tz_ref, o_ref):
    kind = pl.program_id(0)
    for w in range(NA_WIN_ROWS):
        for qp in range(NA_QROWS // 2):
            base = kind * (NA_QROWS * NA_WIN_ROWS) + w
            lo = tz_ref[0, idx_ref[base + (2 * qp) * NA_WIN_ROWS]]
            hi = tz_ref[0, idx_ref[base + (2 * qp + 1) * NA_WIN_ROWS]]
            o_ref[0, 0, w * GRID_W:(w + 1) * GRID_W, qp * LANES:(qp + 1) * LANES] = jnp.concatenate([lo, hi], axis=-1)


def _na_attention(p_lat, p_ctx, bias_tab, rows):
    b, s, _ = p_lat.shape
    lc = p_ctx.shape[1]
    base = (2 * A_QK + A_V) // B_QKV
    nq = NA_QROWS * GRID_W
    n_steps = rows // NA_QROWS
    return pl.pallas_call(
        functools.partial(_na_kernel, rows=rows),
        grid=(b, n_steps),
        in_specs=[
            pl.BlockSpec((1, nq, B_QKV), lambda i, k: (i, k, base)),
            pl.BlockSpec((1, s, B_QKV), lambda i, k: (i, 0, base + 1)),
            pl.BlockSpec((1, s, B_QKV), lambda i, k: (i, 0, base + 2)),
            pl.BlockSpec((1, lc, B_QKV), lambda i, k: (i, 0, base + 1)),
            pl.BlockSpec((1, lc, B_QKV), lambda i, k: (i, 0, base + 2)),
            pl.BlockSpec((1,) + bias_tab.shape[1:], lambda i, k: (_na_bias_kind(k, n_steps), 0, 0, 0)),
        ],
        out_specs=pl.BlockSpec((1, nq, B_QKV), lambda i, k: (i, k, 0)),
        out_shape=jax.ShapeDtypeStruct((b, s, B_QKV), BF),
        scratch_shapes=[pltpu.VMEM((NA_HEADS, NA_WIN_ROWS * GRID_W + lc, nq), F32),
                        pltpu.VMEM((NA_HEADS, NA_WIN_ROWS * GRID_W + lc, nq), BF)],
        compiler_params=_cparams(("arbitrary", "arbitrary")),
        name="na_attn",
    )(p_lat, p_lat, p_lat, p_ctx, p_ctx, bias_tab)


def _dense_ctx_kernel(q_ref, k_ref, v_ref, o_ref, s_ref, p_ref):
    lane = lax.broadcasted_iota(jnp.int32, (q_ref.shape[1], LANES), 1)
    row = lax.broadcasted_iota(jnp.int32, (LANES, q_ref.shape[1]), 0)
    qms, kss, vss = [], [], []
    for hp in range(NA_HEADS // 2):
        cs = slice(hp * LANES, (hp + 1) * LANES)
        q2 = q_ref[0, :, cs]
        zero = jnp.zeros_like(q2)
        qms += [jnp.where(lane < HEAD_DIM, q2, zero), jnp.where(lane >= HEAD_DIM, q2, zero)]
        kss += [[k_ref[0, :, cs]]] * 2
        vss += [[v_ref[0, :, cs]]] * 2
    outs = _attend_phased(qms, kss, vss, [[None]] * NA_HEADS, s_ref, p_ref)
    for hp in range(NA_HEADS // 2):
        o_ref[0, :, hp * LANES:(hp + 1) * LANES] = jnp.where(row < HEAD_DIM, outs[2 * hp], outs[2 * hp + 1]).T.astype(BF)


def _dense_ctx_attention(p_ctx):
    b, lc, _ = p_ctx.shape
    base = (2 * A_QK + A_V) // B_QKV
    spec = lambda j: pl.BlockSpec((1, lc, B_QKV), lambda i: (i, 0, base + j))
    return pl.pallas_call(
        _dense_ctx_kernel,
        grid=(b,),
        in_specs=[spec(0), spec(1), spec(2)],
        out_specs=pl.BlockSpec((1, lc, B_QKV), lambda i: (i, 0, 0)),
        out_shape=jax.ShapeDtypeStruct((b, lc, B_QKV), BF),
        scratch_shapes=[pltpu.VMEM((NA_HEADS, lc, lc), F32), pltpu.VMEM((NA_HEADS, lc, lc), BF)],
        compiler_params=_cparams(("arbitrary",)),
        name="dense_ctx_attn",
    )(p_ctx, p_ctx, p_ctx)


def _mlp_residual(x1, n2g, sc2, sh2, g2, w1_ref, w2_ref):
    h2 = _modulated_norm(x1, n2g, sc2, sh2).astype(BF)
    hc = 1024
    acc = jnp.zeros_like(x1)
    for c in range(MLP_HIDDEN // hc):
        hid = jnp.maximum(_dot(h2, w1_ref[:, c * hc:(c + 1) * hc]), 0.0)
        acc = acc + _dot((hid * hid).astype(BF), w2_ref[c * hc:(c + 1) * hc, :])
    return x1 + g2 * acc


def _even_tail_kernel(oa_ref, ob_ref, x_ref, wo_ref, w1_ref, w2_ref, n2g_ref, g1_ref, sc2_ref, sh2_ref, g2_ref,
                      n1g_ref, sc1n_ref, sh1n_ref, *rest):
    xo_ref, hn_ref, u_ref = rest[-3:]
    y = _dot(oa_ref[0], wo_ref[0:A_V, :]) + _dot(ob_ref[0], wo_ref[A_V:, :])
    x1 = x_ref[0] + g1_ref[0, 0] * y
    x2 = _mlp_residual(x1, n2g_ref[...], sc2_ref[0, 0], sh2_ref[0, 0], g2_ref[0, 0], w1_ref, w2_ref)
    xo_ref[0] = x2
    hn = _modulated_norm(x2, n1g_ref[...], sc1n_ref[0, 0], sh1n_ref[0, 0])
    hn_ref[0] = hn
    n_chunk = hn.shape[0] // SSM_CHUNK
    for j in range(hn.shape[1] // LANES):
        tl = _to_slots(hn[:, j * LANES:(j + 1) * LANES]).reshape(n_chunk, 2, SUBLANES, LANES)
        for o in range(2):
            u_ref[j, o, :, 0] = tl[:, o]


def _even_tail(oa, ob, x, wo, w1, w2, n2g, n1g, m4, layer, row, *, tm, n_chunks, chunk0, u_prev=None):
    b, s, d = x.shape
    cpb = tm // SSM_CHUNK
    u_spec = pl.BlockSpec((d // LANES, 2, cpb, 1, SUBLANES, LANES), lambda i, t: (0, 0, chunk0 // cpb + t, i, 0, 0))
    u_shape = jax.ShapeDtypeStruct((d // LANES, 2, n_chunks, b, SUBLANES, LANES), F32)
    tok = lambda w: pl.BlockSpec((1, tm, w), lambda i, t: (i, t, 0))
    mod = lambda lyr, col: _mod_spec(d, lyr, col, row)
    vec = pl.BlockSpec((1, d), lambda i, t: (0, 0))
    full = lambda a: pl.BlockSpec(a.shape, lambda i, t: (0, 0))
    in_specs = [tok(A_V), tok(B_QKV), tok(d), full(wo), full(w1), full(w2), vec,
                mod(layer, MOD_G1), mod(layer, MOD_SC2), mod(layer, MOD_SH2), mod(layer, MOD_G2), vec,
                mod(layer + 1, MOD_SC1), mod(layer + 1, MOD_SH1)]
    args = [oa, ob, x, wo, w1, w2, n2g, m4, m4, m4, m4, n1g, m4, m4]
    aliases = {}
    if u_prev is not None:
        aliases = {len(args): 2}
        in_specs.append(pl.BlockSpec(memory_space=pl.ANY))
        args.append(u_prev)
    return pl.pallas_call(
        _even_tail_kernel,
        grid=(b, s // tm),
        in_specs=in_specs,
        out_specs=[tok(d), tok(d), u_spec],
        out_shape=[jax.ShapeDtypeStruct((b, s, d), F32), jax.ShapeDtypeStruct((b, s, d), F32), u_shape],
        input_output_aliases=aliases,
        compiler_params=_cparams(("arbitrary", "arbitrary")),
        name="even_tail",
    )(*args)


def _odd_tail_kernel(y_ref, h_ref, x_ref, dsk_ref, wa_ref, wb_ref, w1_ref, w2_ref, n2g_ref, g1_ref, sc2_ref, sh2_ref,
                     g2_ref, fg_ref, o_ref):
    n_chunk = y_ref.shape[2]
    parts = []
    for j in range(y_ref.shape[0]):
        t = jnp.stack([y_ref[j, 0, :, 0], y_ref[j, 1, :, 0]], axis=1).reshape(2 * n_chunk, SUBLANES, LANES)
        parts.append(_from_slots(t))
    z = jnp.concatenate(parts, axis=-1) + dsk_ref[...] * h_ref[0]
    z = jax.nn.gelu(z).astype(BF)
    mix = _dot(z, wa_ref[...]) * jax.nn.sigmoid(_dot(z, wb_ref[...]))
    x1 = x_ref[0] + g1_ref[0, 0] * mix
    x2 = _mlp_residual(x1, n2g_ref[...], sc2_ref[0, 0], sh2_ref[0, 0], g2_ref[0, 0], w1_ref, w2_ref)
    o_ref[0] = _rms(x2, NORM_EPS) * fg_ref[...]


def _odd_tail(y, h, x, dsk, wa, wb, w1, w2, n2g, fg, m4, layer, *, tm):
    b, s, d = x.shape
    tok = pl.BlockSpec((1, tm, d), lambda i, t: (i, t, 0))
    mod = lambda col: _mod_spec(d, layer, col, None)
    vec = pl.BlockSpec((1, d), lambda i, t: (0, 0))
    full = lambda a: pl.BlockSpec(a.shape, lambda i, t: (0, 0))
    return pl.pallas_call(
        _odd_tail_kernel,
        grid=(b, s // tm),
        in_specs=[pl.BlockSpec((d // LANES, 2, tm // SSM_CHUNK, 1, SUBLANES, LANES), lambda i, t: (0, 0, t, i, 0, 0)),
                  tok, tok, vec, full(wa), full(wb), full(w1), full(w2), vec,
                  mod(MOD_G1), mod(MOD_SC2), mod(MOD_SH2), mod(MOD_G2), vec],
        out_specs=tok,
        out_shape=jax.ShapeDtypeStruct((b, s, d), F32),
        compiler_params=_cparams(("arbitrary", "arbitrary")),
        name="odd_tail",
    )(y, h, x, dsk, wa, wb, w1, w2, n2g, m4, m4, m4, m4, fg)


def _s5_ops_kernel(prm_ref, bt_ref, c_ref, toep_ref, win_ref, wst_ref, at_ref):
    t_len = SSM_CHUNK
    p_dim = SSM_STATE
    pair = pl.program_id(0)
    pos = lax.broadcasted_iota(jnp.int32, (t_len, 1), 0)
    row_pos = lax.broadcasted_iota(jnp.int32, (SSM_CW, 1), 0) // SSM_GROUP
    col_pos = lax.broadcasted_iota(jnp.int32, (1, SSM_CW), 1) // SSM_GROUP
    wst_ref[...] = jnp.zeros(wst_ref.shape, wst_ref.dtype)
    zeros_half = jnp.zeros((SSM_CW, p_dim), F32)
    a_rows = [[], [], [], []]
    for jj in range(2):
        m = (2 * pair + jj) % SUBLANES
        m_out = (SUBLANES - m) % SUBLANES

        def tok_in(p):
            return (p // SUBLANES) * SUBLANES + (p % SUBLANES + m) % SUBLANES

        def tok_out(p):
            return (p // SUBLANES) * SUBLANES + (m_out + SUBLANES - p % SUBLANES) % SUBLANES

        e_in = tok_in(pos).astype(F32)
        e_out = tok_out(pos).astype(F32)
        toep = None
        win_units = []
        for d in range(2):
            lr = prm_ref[d, jj, 0:1, :]
            li = prm_ref[d, jj, 1:2, :]
            delta = jnp.exp(prm_ref[d, jj, 2:3, :])
            dec = lr * delta
            ang = li * delta

            def apow(e):
                mag = jnp.exp(dec * e)
                return mag * jnp.cos(ang * e), mag * jnp.sin(ang * e)

            ar1, ai1 = apow(1.0)
            den = lr * lr + li * li
            nr = ar1 - 1.0
            cr = (nr * lr + ai1 * li) / den
            ci = (ai1 * lr - nr * li) / den
            bbr = cr * bt_ref[d, jj, 0] - ci * bt_ref[d, jj, 1]
            bbi = cr * bt_ref[d, jj, 1] + ci * bt_ref[d, jj, 0]
            cre = c_ref[d, jj, 0]
            cim = c_ref[d, jj, 1]

            def factor(e, xr, xi):
                ar, ai = apow(e)
                re = ar[:, None, :] * xr[None] - ai[:, None, :] * xi[None]
                im = ar[:, None, :] * xi[None] + ai[:, None, :] * xr[None]
                return re.reshape(SSM_CW, p_dim), im.reshape(SSM_CW, p_dim)

            sgn = 1.0 if d == 0 else -1.0
            lre, lim = factor(-sgn * e_in, bbr, bbi)
            rre, rim = factor(sgn * e_out, cre, cim)
            full = lax.dot_general(jnp.concatenate([lre, lim], axis=-1), jnp.concatenate([rre, -rim], axis=-1),
                                   (((1,), (1,)), ((), ())), precision=lax.Precision.HIGHEST,
                                   preferred_element_type=F32)
            if d == 0:
                valid = tok_out(col_pos) >= tok_in(row_pos)
            else:
                valid = tok_in(row_pos) >= tok_out(col_pos)
            part = jnp.where(valid, full, 0.0)
            toep = part if toep is None else toep + part

            wre, wim = factor((t_len - 1) - e_in, bbr, bbi) if d == 0 else (lre, lim)
            for x in (wre, wim):
                win_units.append(jnp.concatenate([x, zeros_half] if jj == 0 else [zeros_half, x], axis=-1))
            sre, sim = factor(e_out + 1.0 if d == 0 else t_len - e_out, cre, cim)
            st = jnp.concatenate([sre, -sim], axis=-1).T
            for comp in range(2):
                r0 = (2 * d + comp) * 2 * p_dim + jj * p_dim
                wst_ref[jj, r0:r0 + p_dim, :] = st[comp * p_dim:(comp + 1) * p_dim].astype(BF)
            atr, ati = apow(float(t_len))
            a_rows[2 * d].append(atr)
            a_rows[2 * d + 1].append(ati)
        toep_ref[jj] = toep.astype(BF)
        win_ref[jj] = jnp.concatenate(win_units, axis=-1).astype(BF)
    at_ref[0] = jnp.concatenate([jnp.concatenate(r, axis=-1) for r in a_rows], axis=0)


def _s5_operators(lam_re, lam_im, log_step, b_re, b_im, c_re, c_im):
    q = np.arange(SSM_GROUPS)
    group = (q // SUBLANES) * SUBLANES + (SUBLANES - q % SUBLANES) % SUBLANES
    g = SSM_GROUPS
    p_dim = SSM_STATE
    prm = jnp.stack([lam_re.astype(F32), lam_im.astype(F32),
                     jnp.broadcast_to(log_step.astype(F32)[..., None], lam_re.shape)], axis=2)[:, group]
    bt = jnp.stack([b_re.astype(F32), b_im.astype(F32)], axis=2).transpose(0, 1, 2, 4, 3)[:, group]
    cc = jnp.stack([c_re.astype(F32), c_im.astype(F32)], axis=2)[:, group]
    cw = SSM_CW
    return pl.pallas_call(
        _s5_ops_kernel,
        grid=(g // 2,),
        in_specs=[pl.BlockSpec((2, 2, 3, p_dim), lambda i: (0, i, 0, 0)),
                  pl.BlockSpec((2, 2, 2, SSM_GROUP, p_dim), lambda i: (0, i, 0, 0, 0)),
                  pl.BlockSpec((2, 2, 2, SSM_GROUP, p_dim), lambda i: (0, i, 0, 0, 0))],
        out_specs=[pl.BlockSpec((2, cw, cw), lambda i: (i, 0, 0)),
                   pl.BlockSpec((2, cw, 8 * p_dim), lambda i: (i, 0, 0)),
                   pl.BlockSpec((2, 8 * p_dim, cw), lambda i: (i, 0, 0)),
                   pl.BlockSpec((1, 4, 2 * p_dim), lambda i: (i, 0, 0))],
        out_shape=[jax.ShapeDtypeStruct((g, cw, cw), BF), jax.ShapeDtypeStruct((g, cw, 8 * p_dim), BF),
                   jax.ShapeDtypeStruct((g, 8 * p_dim, cw), BF), jax.ShapeDtypeStruct((g // 2, 4, 2 * p_dim), F32)],
        compiler_params=_cparams(("arbitrary",)),
        name="s5_ops",
    )(prm, bt, cc)


def _to_slots(x):
    a = pltpu.roll(x, 0, 1, stride=SSM_GROUP, stride_axis=0)
    a = a.reshape(x.shape[0] // SUBLANES, SUBLANES, LANES)
    blk = lax.broadcasted_iota(jnp.int32, a.shape, 2) // SSM_GROUP
    for bit in (1, 2, 4):
        a = jnp.where((blk & bit) != 0, pltpu.roll(a, SUBLANES - bit, 1), a)
    return a


def _from_slots(t):
    blk = lax.broadcasted_iota(jnp.int32, t.shape, 2) // SSM_GROUP
    for bit in (1, 2, 4):
        t = jnp.where((blk & bit) != 0, pltpu.roll(t, SUBLANES - bit, 1), t)
    a = t.reshape(t.shape[0] * SUBLANES, LANES)
    return pltpu.roll(a, 0, 1, stride=SSM_GROUP, stride_axis=0)


def _s5_kernel(u_ref, toep_ref, win_ref, wst_ref, at_ref, y_ref, sloc_ref, sin_ref, *, n_ctx, n_lat, nb):
    p = pl.program_id(1)
    w = 2 * SSM_STATE
    n_rows = (n_ctx + n_lat) * nb
    ctx0 = n_lat * nb
    slots = [2 * p, 2 * p + 1]
    us = [jnp.concatenate([u_ref[0, o, pl.ds(m, n_rows, stride=SUBLANES), :] for o in range(2)], axis=-1).astype(BF)
          for m in slots]
    sloc_ref[...] = _dot(us[0], win_ref[0]) + _dot(us[1], win_ref[1])

    afr = jnp.broadcast_to(at_ref[0, 0:1, :], (nb, w))
    afi = jnp.broadcast_to(at_ref[0, 1:2, :], (nb, w))
    abr = jnp.broadcast_to(at_ref[0, 2:3, :], (nb, w))
    abi = jnp.broadcast_to(at_ref[0, 3:4, :], (nb, w))

    def advance(carry, rf, rb):
        fr, fi, br, bi = carry
        xfr = sloc_ref[pl.ds(rf, nb), 0:w]
        xfi = sloc_ref[pl.ds(rf, nb), w:2 * w]
        xbr = sloc_ref[pl.ds(rb, nb), 2 * w:3 * w]
        xbi = sloc_ref[pl.ds(rb, nb), 3 * w:4 * w]
        return (afr * fr - afi * fi + xfr, afr * fi + afi * fr + xfi,
                abr * br - abi * bi + xbr, abr * bi + abi * br + xbi)

    def ctx_body(i, carry):
        rf = pl.multiple_of(ctx0 + i * nb, nb)
        rb = pl.multiple_of(ctx0 + (n_ctx - 1 - i) * nb, nb)
        return advance(carry, rf, rb)

    def lat_body(i, carry):
        fr, fi, br, bi = carry
        rf = pl.multiple_of(i * nb, nb)
        rb = pl.multiple_of((n_lat - 1 - i) * nb, nb)
        sin_ref[pl.ds(rf, nb), 0:w] = fr
        sin_ref[pl.ds(rf, nb), w:2 * w] = fi
        sin_ref[pl.ds(rb, nb), 2 * w:3 * w] = br
        sin_ref[pl.ds(rb, nb), 3 * w:4 * w] = bi
        return advance(carry, rf, rb)

    zero = jnp.zeros((nb, w), F32)
    carry = lax.fori_loop(0, n_ctx, ctx_body, (zero, zero, zero, zero))
    lax.fori_loop(0, n_lat, lat_body, carry)

    s_in = sin_ref[...].astype(BF)
    for q in range(2):
        y = _dot(us[q][:ctx0], toep_ref[q]) + _dot(s_in, wst_ref[q])
        m_out = (SUBLANES - slots[q]) % SUBLANES
        for o in range(2):
            y_ref[0, o, pl.ds(m_out, n_lat * nb, stride=SUBLANES), :] = y[:, o * LANES:(o + 1) * LANES]


def _s5_scan(u, toep, w_in, w_st, a_t, *, n_ctx, nb):
    nj, _, r8, _ = u.shape
    n_lat = r8 // (nb * SUBLANES) - n_ctx
    cw = SSM_CW
    n_pairs = SUBLANES // 2
    pair = lambda r, c: pl.BlockSpec((2, r, c), lambda j, p: (j * n_pairs + p, 0, 0))
    return pl.pallas_call(
        functools.partial(_s5_kernel, n_ctx=n_ctx, n_lat=n_lat, nb=nb),
        grid=(nj, n_pairs),
        in_specs=[pl.BlockSpec((1, 2, r8, LANES), lambda j, p: (j, 0, 0, 0)),
                  pair(cw, cw), pair(cw, 8 * SSM_STATE), pair(8 * SSM_STATE, cw),
                  pl.BlockSpec((1, 4, 2 * SSM_STATE), lambda j, p: (j * n_pairs + p, 0, 0))],
        out_specs=pl.BlockSpec((1, 2, n_lat * nb * SUBLANES, LANES), lambda j, p: (j, 0, 0, 0)),
        out_shape=jax.ShapeDtypeStruct((nj, 2, n_lat * nb * SUBLANES, LANES), F32),
        scratch_shapes=[pltpu.VMEM(((n_ctx + n_lat) * nb, 8 * SSM_STATE), F32),
                        pltpu.VMEM((n_lat * nb, 8 * SSM_STATE), F32)],
        compiler_params=_cparams(("arbitrary", "arbitrary")),
        name="s5_scan",
    )(u, toep, w_in, w_st, a_t)


def kernel(x, c, ctx, c_ctx, w_ada, b_ada, norm1_g, norm2_g, final_g, w_in, w_out, lam_q1, lam_k1, lam_q2, lam_k2,
           subln_g, na_rpb, ssm_lam_re, ssm_lam_im, ssm_log_step, ssm_b_re, ssm_b_im, ssm_c_re, ssm_c_im, ssm_d,
           glu_w_a, glu_w_b, mlp_w1, mlp_w2):
    b, s, d = x.shape
    lc = ctx.shape[1]
    rows = s // GRID_W
    assert DEPTH == 2 and d == D_MODEL and s % GRID_W == 0 and rows >= NA_WIN_ROWS and rows % NA_QROWS == 0
    assert s % SSM_CHUNK == 0 and lc % SSM_CHUNK == 0 and b % SUBLANES == 0

    n_act = -(-(b + 1) // SUBLANES) * SUBLANES
    act_in = jnp.zeros((n_act, d), F32).at[:b].set(c).at[b].set(c_ctx)
    m4 = _ada(act_in, w_ada, b_ada).reshape(DEPTH, n_act, 1, 6 * d)
    n1g = [norm1_g[i].reshape(1, d) for i in range(DEPTH)]
    n2g = [norm2_g[i].reshape(1, d) for i in range(DEPTH)]

    cos_t, sin_t = _rope_tables(s)
    w_in_b = w_in[0].astype(BF)
    p_lat = _inproj(x, n1g[0], m4, 0, None, w_in_b, cos_t, sin_t, rope=True, tm=1024)
    p_ctx = _inproj(ctx, n1g[0], m4, 0, b, w_in_b, cos_t[:lc], sin_t[:lc], rope=False, tm=lc)

    lam_init = 0.8 - 0.6 * math.exp(-0.3 * 0)
    lam_p = jnp.stack([lam_q1[0], lam_k1[0], lam_q2[0], lam_k2[0]]).astype(F32)
    gsub = subln_g[0].reshape(1, 2 * HEAD_DIM).astype(F32)
    oa = _diff_attention(lam_p, gsub, p_lat, [p_lat, p_ctx], tq=1024, lam_init=lam_init, name="diff_attn")
    oac = _diff_attention(lam_p, gsub, p_ctx, [p_ctx], tq=lc, lam_init=lam_init, name="diff_attn_ctx")
    ob = _na_attention(p_lat, p_ctx, _na_bias_table(na_rpb[0], rows), rows)
    obc = _dense_ctx_attention(p_ctx)

    wo_b = w_out[0].astype(BF)
    w1_b = [mlp_w1[i].astype(BF) for i in range(DEPTH)]
    w2_b = [mlp_w2[i].astype(BF) for i in range(DEPTH)]
    n_lat, n_ctx = s // SSM_CHUNK, lc // SSM_CHUNK
    x1, h1, u = _even_tail(oa, ob, x, wo_b, w1_b[0], w2_b[0], n2g[0], n1g[1], m4, 0, None, tm=512,
                           n_chunks=n_lat + n_ctx, chunk0=0)
    _, _, u = _even_tail(oac, obc, ctx, wo_b, w1_b[0], w2_b[0], n2g[0], n1g[1], m4, 0, b, tm=lc,
                         n_chunks=n_lat + n_ctx, chunk0=n_lat, u_prev=u)

    toep, w_si, w_st, a_t = _s5_operators(ssm_lam_re[0], ssm_lam_im[0], ssm_log_step[0], ssm_b_re[0], ssm_b_im[0],
                                          ssm_c_re[0], ssm_c_im[0])
    nj = d // LANES
    y = _s5_scan(u.reshape(nj, 2, (n_lat + n_ctx) * b * SUBLANES, LANES), toep, w_si, w_st, a_t, n_ctx=n_ctx, nb=b)
    y = y.reshape(nj, 2, n_lat, b, SUBLANES, LANES)
    return _odd_tail(y, h1, x1, ssm_d[0].reshape(1, d).astype(F32), glu_w_a[0].astype(BF), glu_w_b[0].astype(BF),
                     w1_b[1], w2_b[1], n2g[1], final_g.reshape(1, d).astype(F32), m4, 1, tm=512)
```
